```python
import math
import jax, jax.numpy as jnp
from jax import lax
import numpy as np

D_MODEL = 1024
BATCH = 8
SEQ = 2048
DEPTH = 1

SSD_EXPAND = 2
SSD_D_INNER = SSD_EXPAND * D_MODEL
SSD_HEAD_DIM = 64
SSD_HEADS = SSD_D_INNER // SSD_HEAD_DIM
SSD_GROUPS = 4
SSD_HEADS_PER_GROUP = SSD_HEADS // SSD_GROUPS
SSD_STATE = 128
SSD_CONV = 4
SSD_CHUNK = 128
SSD_CONV_CH = SSD_D_INNER + 2 * SSD_GROUPS * SSD_STATE
SSD_DT_MIN = 0.001
SSD_DT_MAX = 0.1

DA_HEAD_DIM = 64
DA_HEADS = D_MODEL // (2 * DA_HEAD_DIM)
DA_V_DIM = 2 * DA_HEAD_DIM
DA_WIDTH = DA_HEADS * 2 * DA_HEAD_DIM
Q_BLOCK = 128

N_BRANCH = 2
D_FF = 2816
EPS = 1e-6

IN_SIZES = [SSD_D_INNER, SSD_CONV_CH, SSD_HEADS, DA_WIDTH, DA_WIDTH, DA_WIDTH, N_BRANCH * D_MODEL]
D_IN_PROJ = sum(IN_SIZES)
IN_SPLITS = np.cumsum(IN_SIZES)[:-1].tolist()

kernel_name = "hybrid_ssd_diffattn_macaron_block"


def rmsnorm(x, g):
    xf = x.astype(jnp.float32)
    y = xf * lax.rsqrt(jnp.mean(xf * xf, axis=-1, keepdims=True) + EPS)
    return y.astype(x.dtype) * g


def swiglu(x, w_gate, w_up, w_down):
    return (jax.nn.silu(x @ w_gate) * (x @ w_up)) @ w_down


def alibi_slopes(n_heads):
    return jnp.asarray(2.0 ** (-8.0 * np.arange(1, n_heads + 1) / n_heads), dtype=jnp.float32)


def causal_dwconv(x, w, b):
    k = w.shape[0]
    y = lax.conv_general_dilated(x, w[:, None, :], window_strides=(1,), padding=[(k - 1, 0)],
                                 dimension_numbers=("NWC", "WIO", "NWC"),
                                 feature_group_count=x.shape[-1])
    return y + b


def ssd_chunked(x, dt, a_head, bm, cm):
    b, L, H, P = x.shape
    G, N, E, Q = SSD_GROUPS, SSD_STATE, SSD_HEADS_PER_GROUP, SSD_CHUNK
    nc = L // Q
    X = (x.astype(jnp.float32) * dt[..., None]).reshape(b, nc, Q, G, E, P)
    a = (dt * a_head).reshape(b, nc, Q, G, E).transpose(0, 3, 4, 1, 2)
    Bc = bm.astype(jnp.float32).reshape(b, nc, Q, G, N)
    Cc = cm.astype(jnp.float32).reshape(b, nc, Q, G, N)
    a_cum = jnp.cumsum(a, axis=-1)
    seg = a_cum[..., :, None] - a_cum[..., None, :]
    causal = jnp.tril(jnp.ones((Q, Q), dtype=bool))
    Lmat = jnp.exp(jnp.where(causal, seg, -jnp.inf))
    CB = jnp.einsum('bclgn,bcsgn->bgcls', Cc, Bc)
    y_diag = jnp.einsum('bgcls,bgecls,bcsgep->bclgep', CB, Lmat, X)
    decay_states = jnp.exp(a_cum[..., -1:] - a_cum)
    states = jnp.einsum('bclgn,bgecl,bclgep->bcgepn', Bc, decay_states, X)
    chunk_decay = jnp.exp(a_cum[..., -1])

    def step(h, inp):
        s, d = inp
        return h * d[..., None, None] + s, h

    _, prev = lax.scan(step, jnp.zeros(states.shape[:1] + states.shape[2:], jnp.float32),
                       (jnp.moveaxis(states, 1, 0), jnp.moveaxis(chunk_decay, 3, 0)))
    prev = jnp.moveaxis(prev, 0, 1)
    y_off = jnp.einsum('bclgn,bcgepn,bgecl->bclgep', Cc, prev, jnp.exp(a_cum))
    return (y_diag + y_off).reshape(b, L, H, P)


def ssd_branch(z, xbc, dt_raw, conv_w, conv_b, dt_bias, a_log, d_skip, norm_g, w_branch):
    b, L, _ = z.shape
    xbc = jax.nn.silu(causal_dwconv(xbc, conv_w, conv_b))
    xs, bm, cm = jnp.split(xbc, [SSD_D_INNER, SSD_D_INNER + SSD_GROUPS * SSD_STATE], axis=-1)
    dt = jax.nn.softplus(dt_raw.astype(jnp.float32) + dt_bias.astype(jnp.float32))
    a_head = -jnp.exp(a_log.astype(jnp.float32))
    xh = xs.reshape(b, L, SSD_HEADS, SSD_HEAD_DIM)
    y = ssd_chunked(xh, dt, a_head,
                    bm.reshape(b, L, SSD_GROUPS, SSD_STATE), cm.reshape(b, L, SSD_GROUPS, SSD_STATE))
    y = (y + d_skip.astype(jnp.float32)[:, None] * xh.astype(jnp.float32)).astype(z.dtype)
    y = y.reshape(b, L, SSD_D_INNER) * jax.nn.silu(z)
    y = rmsnorm(y.reshape(b, L, SSD_GROUPS, SSD_D_INNER // SSD_GROUPS), 1.0).reshape(b, L, SSD_D_INNER) * norm_g
    return y @ w_branch


def diff_attn_branch(q, k, v, lq1, lk1, lq2, lk2, subln_g, w_branch, lam_init):
    b, L, _ = q.shape
    H, d = DA_HEADS, DA_HEAD_DIM
    q = q.reshape(b, L, H, 2, d)
    k = k.reshape(b, L, H, 2, d)
    v = v.reshape(b, L, H, DA_V_DIM)
    lam = (jnp.exp(jnp.sum(lq1.astype(jnp.float32) * lk1.astype(jnp.float32)))
           - jnp.exp(jnp.sum(lq2.astype(jnp.float32) * lk2.astype(jnp.float32))) + lam_init)
    slopes = alibi_slopes(H)
    scale = 1.0 / math.sqrt(d)
    nb = L // Q_BLOCK
    qb = q.reshape(b, nb, Q_BLOCK, H, 2, d).transpose(1, 0, 2, 3, 4, 5)
    key_pos = jnp.arange(L)

    def one_block(args):
        q_blk, i = args
        q_pos = i * Q_BLOCK + jnp.arange(Q_BLOCK)
        s = jnp.einsum('bqhmd,bkhmd->bhmqk', q_blk, k).astype(jnp.float32) * scale
        dist = (q_pos[:, None] - key_pos[None, :]).astype(jnp.float32)
        s = s - slopes[None, :, None, None, None] * dist
        s = jnp.where(dist >= 0, s, -jnp.inf)
        p = jax.nn.softmax(s, axis=-1)
        attn = p[:, :, 0] - lam * p[:, :, 1]
        return jnp.einsum('bhqk,bkhe->bqhe', attn.astype(v.dtype), v)

    out = lax.map(one_block, (qb, jnp.arange(nb)))
    out = out.transpose(1, 0, 2, 3, 4).reshape(b, L, H, DA_V_DIM)
    out = rmsnorm(out, subln_g) * (1.0 - lam_init)
    return out.reshape(b, L, DA_WIDTH) @ w_branch


def setup_inputs(seed: int = 0) -> dict:
    key = jax.random.key(seed)
    ks = iter(jax.random.split(key, 40))
    f32 = jnp.float32

    def nrm(shape, fan_in):
        return jax.random.normal(next(ks), (DEPTH,) + shape, f32) * fan_in ** -0.5

    def gain(n):
        return 1.0 + 0.05 * jax.random.normal(next(ks), (DEPTH, n), f32)

    x = jax.random.normal(next(ks), (BATCH, SEQ, D_MODEL), f32)
    dt0 = jnp.exp(jax.random.uniform(next(ks), (DEPTH, SSD_HEADS), f32)
                  * (math.log(SSD_DT_MAX) - math.log(SSD_DT_MIN)) + math.log(SSD_DT_MIN))
    dt0 = jnp.maximum(dt0, 1e-4)
    return {
        "x": x,
        "ffn1_pre_g": gain(D_MODEL),
        "ffn1_w_gate": nrm((D_MODEL, D_FF), D_MODEL),
        "ffn1_w_up": nrm((D_MODEL, D_FF), D_MODEL),
        "ffn1_w_down": nrm((D_FF, D_MODEL), D_FF),
        "ffn1_post_g": gain(D_MODEL),
        "mix_pre_g": gain(D_MODEL),
        "w_in": nrm((D_MODEL, D_IN_PROJ), D_MODEL),
        "gate_b": 0.02 * jax.random.normal(next(ks), (DEPTH, N_BRANCH * D_MODEL), f32),
        "ssd_conv_w": nrm((SSD_CONV, SSD_CONV_CH), SSD_CONV),
        "ssd_conv_b": 0.02 * jax.random.normal(next(ks), (DEPTH, SSD_CONV_CH), f32),
        "ssd_dt_bias": dt0 + jnp.log(-jnp.expm1(-dt0)),
        "ssd_A_log": jnp.log(jax.random.uniform(next(ks), (DEPTH, SSD_HEADS), f32, 1.0, 16.0)),
        "ssd_D": gain(SSD_HEADS),
        "ssd_norm_g": gain(SSD_D_INNER),
        "ssd_w_branch": nrm((SSD_D_INNER, D_MODEL), SSD_D_INNER),
        "da_lambda_q1": 0.1 * jax.random.normal(next(ks), (DEPTH, DA_HEAD_DIM), f32),
        "da_lambda_k1": 0.1 * jax.random.normal(next(ks), (DEPTH, DA_HEAD_DIM), f32),
        "da_lambda_q2": 0.1 * jax.random.normal(next(ks), (DEPTH, DA_HEAD_DIM), f32),
        "da_lambda_k2": 0.1 * jax.random.normal(next(ks), (DEPTH, DA_HEAD_DIM), f32),
        "da_subln_g": gain(DA_V_DIM),
        "da_w_branch": nrm((DA_WIDTH, D_MODEL), DA_WIDTH),
        "w_out": nrm((D_MODEL, D_MODEL), D_MODEL),
        "mix_post_g": gain(D_MODEL),
        "ffn2_pre_g": gain(D_MODEL),
        "ffn2_w_gate": nrm((D_MODEL, D_FF), D_MODEL),
        "ffn2_w_up": nrm((D_MODEL, D_FF), D_MODEL),
        "ffn2_w_down": nrm((D_FF, D_MODEL), D_FF),
        "ffn2_post_g": gain(D_MODEL),
    }


def reference(x, ffn1_pre_g, ffn1_w_gate, ffn1_w_up, ffn1_w_down, ffn1_post_g,
              mix_pre_g, w_in, gate_b,
              ssd_conv_w, ssd_conv_b, ssd_dt_bias, ssd_A_log, ssd_D, ssd_norm_g, ssd_w_branch,
              da_lambda_q1, da_lambda_k1, da_lambda_q2, da_lambda_k2, da_subln_g, da_w_branch,
              w_out, mix_post_g,
              ffn2_pre_g, ffn2_w_gate, ffn2_w_up, ffn2_w_down, ffn2_post_g):
    h = x
    b, L, _ = x.shape
    for l in range(DEPTH):
        lam_init = 0.8 - 0.6 * math.exp(-0.3 * l)
        h = h + 0.5 * rmsnorm(swiglu(rmsnorm(h, ffn1_pre_g[l]), ffn1_w_gate[l], ffn1_w_up[l], ffn1_w_down[l]),
                              ffn1_post_g[l])
        u = rmsnorm(h, mix_pre_g[l])
        z, xbc, dt_raw, q, k, v, gate_logits = jnp.split(u @ w_in[l], IN_SPLITS, axis=-1)
        y_ssd = ssd_branch(z, xbc, dt_raw, ssd_conv_w[l], ssd_conv_b[l], ssd_dt_bias[l], ssd_A_log[l],
                           ssd_D[l], ssd_norm_g[l], ssd_w_branch[l])
        y_att = diff_attn_branch(q, k, v, da_lambda_q1[l], da_lambda_k1[l], da_lambda_q2[l], da_lambda_k2[l],
                                 da_subln_g[l], da_w_branch[l], lam_init)
        gates = jax.nn.sigmoid(gate_logits + gate_b[l]).reshape(b, L, N_BRANCH, D_MODEL)
        merged = gates[:, :, 0] * y_ssd + gates[:, :, 1] * y_att
        h = h + rmsnorm(merged @ w_out[l], mix_post_g[l])
        h = h + 0.5 * rmsnorm(swiglu(rmsnorm(h, ffn2_pre_g[l]), ffn2_w_gate[l], ffn2_w_up[l], ffn2_w_down[l]),
                              ffn2_post_g[l])
    return h
```

```python
import functools
import math

import jax
import jax.numpy as jnp
from jax import lax
from jax.experimental import pallas as pl
from jax.experimental.pallas import tpu as pltpu

F32 = jnp.float32
BF16 = jnp.bfloat16

EPS = 1e-6

SSD_HEAD_DIM = 64
SSD_GROUPS = 4
SSD_STATE = 128
SSD_CHUNK = 128
DA_HEAD_DIM = 64
N_BRANCH = 2

V7X_LANES = 128
V7X_SUBLANES = 8
V7X_VMEM_BYTES = 64 * 1024 * 1024
MIB = 1024 * 1024

NEG_BIG = -1e30


def _const_spec(shape):
    nd = len(shape)
    return pl.BlockSpec(shape, lambda *_: (0,) * nd, pipeline_mode=pl.Buffered(1))


def _rms(x):
    return x * lax.rsqrt(jnp.mean(x * x, axis=-1, keepdims=True) + EPS)


def _silu(x):
    return x * jax.nn.sigmoid(x)


def _softplus(x):
    return jnp.maximum(x, 0.0) + jnp.log(1.0 + jnp.exp(-jnp.abs(x)))


def _ffn_kernel(x_ref, pre_g_ref, wg_ref, wu_ref, wd_ref, post_g_ref, o_ref):
    x = x_ref[...]
    xn = (_rms(x) * pre_g_ref[...]).astype(BF16)
    g = jnp.dot(xn, wg_ref[...], preferred_element_type=F32)
    u = jnp.dot(xn, wu_ref[...], preferred_element_type=F32)
    a = (_silu(g) * u).astype(BF16)
    y = jnp.dot(a, wd_ref[...], preferred_element_type=F32)
    o_ref[...] = x + 0.5 * (_rms(y) * post_g_ref[...])


def _ffn(x, pre_g, wg, wu, wd, post_g, *, tm):
    t, d = x.shape
    f = wg.shape[1]
    vmem = (3 * d * f * 2) + 4 * tm * d * 4 + 3 * tm * f * 4 + 8 * MIB
    return pl.pallas_call(
        _ffn_kernel,
        grid=(t // tm,),
        in_specs=[
            pl.BlockSpec((tm, d), lambda i: (i, 0)),
            _const_spec((1, d)),
            _const_spec((d, f)),
            _const_spec((d, f)),
            _const_spec((f, d)),
            _const_spec((1, d)),
        ],
        out_specs=pl.BlockSpec((tm, d), lambda i: (i, 0)),
        out_shape=jax.ShapeDtypeStruct((t, d), F32),
        compiler_params=pltpu.CompilerParams(
            dimension_semantics=("arbitrary",), vmem_limit_bytes=min(vmem, V7X_VMEM_BYTES - 4 * MIB)),
        name="ffn",
    )(x, pre_g, wg, wu, wd, post_g)


def _in_proj_kernel(h_ref, g_ref, wz_ref, wx_ref, wdt_ref, wq_ref, wk_ref, wv_ref, wgl_ref,
                    z_ref, xbc_ref, dt_ref, q_ref, k_ref, v_ref, gl_ref, *, q_scale):
    u = (_rms(h_ref[...]) * g_ref[...]).astype(BF16)

    def proj(w_ref):
        return jnp.dot(u, w_ref[...], preferred_element_type=F32)

    z_ref[...] = proj(wz_ref)
    xbc_ref[...] = proj(wx_ref)
    dt_ref[...] = proj(wdt_ref)
    q_ref[...] = (proj(wq_ref) * q_scale).astype(BF16)
    k_ref[...] = proj(wk_ref).astype(BF16)
    v_ref[...] = proj(wv_ref).astype(BF16)
    gl_ref[...] = proj(wgl_ref)


def _in_proj(h, g, ws, *, tm, q_scale):
    t, d = h.shape
    widths = [w.shape[1] for w in ws]
    out_dtypes = [F32, F32, F32, BF16, BF16, BF16, F32]
    w_bytes = sum(d * n * 2 for n in widths)
    o_bytes = sum(tm * n * jnp.dtype(dt).itemsize for n, dt in zip(widths, out_dtypes))
    vmem = w_bytes + 2 * o_bytes + 2 * tm * d * 4 + tm * max(widths) * 4 + 8 * MIB
    return pl.pallas_call(
        functools.partial(_in_proj_kernel, q_scale=q_scale),
        grid=(t // tm,),
        in_specs=[pl.BlockSpec((tm, d), lambda i: (i, 0)), _const_spec((1, d))]
        + [_const_spec(w.shape) for w in ws],
        out_specs=[pl.BlockSpec((tm, n), lambda i: (i, 0)) for n in widths],
        out_shape=[jax.ShapeDtypeStruct((t, n), dt) for n, dt in zip(widths, out_dtypes)],
        compiler_params=pltpu.CompilerParams(
            dimension_semantics=("arbitrary",), vmem_limit_bytes=min(vmem, V7X_VMEM_BYTES - 4 * MIB)),
        name="in_proj",
    )(h, g, *ws)


def _ssd_kernel(xbc_ref, z_ref, dt_ref, cw_ref, cb_ref, dtb_ref, alog_ref, dsk_ref, ng_ref,
                o_ref, xwin_ref, state_ref, y_ref, *, n_heads, d_inner, conv_k):
    q = SSD_CHUNK
    p = SSD_HEAD_DIM
    n = SSD_STATE
    halo = V7X_SUBLANES
    heads_per_group = n_heads // SSD_GROUPS
    pairs_per_group = heads_per_group // 2
    group_w = d_inner // SSD_GROUPS
    c = pl.program_id(1)

    @pl.when(c == 0)
    def _():
        xwin_ref[0:halo, :] = jnp.zeros((halo, xwin_ref.shape[1]), F32)
        state_ref[...] = jnp.zeros(state_ref.shape, F32)

    xwin_ref[halo:halo + q, :] = xbc_ref[0]
    acc = cb_ref[...] + cw_ref[conv_k - 1:conv_k, :] * xwin_ref[halo:halo + q, :]
    for j in range(1, conv_k):
        acc = acc + cw_ref[conv_k - 1 - j:conv_k - j, :] * xwin_ref[halo - j:halo - j + q, :]
    xwin_ref[0:halo, :] = xwin_ref[q:q + halo, :]
    xc = _silu(acc)

    dt = _softplus(dt_ref[0] + dtb_ref[...])
    a = dt * (-jnp.exp(alog_ref[...]))
    row = lax.broadcasted_iota(jnp.int32, (q, q), 0)
    col = lax.broadcasted_iota(jnp.int32, (q, q), 1)
    causal = row >= col
    tril = jnp.where(causal, 1.0, 0.0).astype(F32)
    acum = jnp.dot(tril, a, preferred_element_type=F32, precision=lax.Precision.HIGHEST)
    acum_t = acum.T
    dt_t = dt.T
    a_last_t = jnp.broadcast_to(acum_t[:, q - 1:q], acum_t.shape)
    w_t = jnp.exp(a_last_t - acum_t) * dt_t
    cdecay_t = jnp.exp(a_last_t)

    lane = lax.broadcasted_iota(jnp.int32, (q, 2 * p), 1)
    first = lane < p

    for g in range(SSD_GROUPS):
        b_g = xc[:, d_inner + g * n:d_inner + (g + 1) * n]
        c_g = xc[:, d_inner + SSD_GROUPS * n + g * n:d_inner + SSD_GROUPS * n + (g + 1) * n]
        bt_g = b_g.T
        cb = jnp.dot(c_g.astype(BF16), bt_g.astype(BF16), preferred_element_type=F32)
        for j in range(pairs_per_group):
            pair = g * pairs_per_group + j
            h0 = 2 * pair
            cols = slice(pair * 2 * p, (pair + 1) * 2 * p)
            x_pair = xc[:, cols]
            m_parts, btw_parts, cs_parts = [], [], []
            for hh in (h0, h0 + 1):
                col_a = jnp.broadcast_to(acum[:, hh:hh + 1], (q, q))
                seg = col_a - acum_t[hh:hh + 1, :]
                lmat = jnp.exp(jnp.where(causal, seg, NEG_BIG))
                m_parts.append((cb * lmat * dt_t[hh:hh + 1, :]).astype(BF16))
                btw_parts.append((bt_g * w_t[hh:hh + 1, :]).astype(BF16))
                cs_parts.append((c_g * jnp.exp(col_a)).astype(BF16))
            xz = jnp.concatenate([jnp.where(first, x_pair, 0.0), jnp.where(first, 0.0, x_pair)],
                                 axis=0).astype(BF16)
            lhs = jnp.concatenate([jnp.concatenate(m_parts, axis=1),
                                   jnp.concatenate(btw_parts, axis=1)], axis=0)
            r1 = jnp.dot(lhs, xz, preferred_element_type=F32)
            prev = state_ref[pair]
            pz = jnp.concatenate([jnp.where(first, prev, 0.0), jnp.where(first, 0.0, prev)],
                                 axis=0).astype(BF16)
            y_off = jnp.dot(jnp.concatenate(cs_parts, axis=1), pz, preferred_element_type=F32)
            cd = jnp.where(first[0:1, :], cdecay_t[h0:h0 + 1, :], cdecay_t[h0 + 1:h0 + 2, :])
            state_ref[pair] = prev * cd + r1[q:q + n, :]
            y = r1[0:q, :] + y_off + dsk_ref[:, cols] * x_pair
            y_ref[:, cols] = y * _silu(z_ref[0, :, cols])

    for g in range(SSD_GROUPS):
        gs = slice(g * group_w, (g + 1) * group_w)
        o_ref[0, :, gs] = (_rms(y_ref[:, gs]) * ng_ref[:, gs]).astype(o_ref.dtype)


def _ssd(xbc, z, dtp, conv_w, conv_b, dt_bias, a_log, d_skip, norm_g, *, n_heads):
    b, l, cch = xbc.shape
    d_inner = z.shape[-1]
    q = SSD_CHUNK
    conv_k = conv_w.shape[0]
    n_pairs = n_heads // 2
    kern = functools.partial(_ssd_kernel, n_heads=n_heads, d_inner=d_inner, conv_k=conv_k)
    return pl.pallas_call(
        kern,
        grid=(b, l // q),
        in_specs=[
            pl.BlockSpec((1, q, cch), lambda i, j: (i, j, 0)),
            pl.BlockSpec((1, q, d_inner), lambda i, j: (i, j, 0)),
            pl.BlockSpec((1, q, V7X_LANES), lambda i, j: (i, j, 0)),
            _const_spec(conv_w.shape),
            _const_spec(conv_b.shape),
            _const_spec(dt_bias.shape),
            _const_spec(a_log.shape),
            _const_spec(d_skip.shape),
            _const_spec(norm_g.shape),
        ],
        out_specs=pl.BlockSpec((1, q, d_inner), lambda i, j: (i, j, 0)),
        out_shape=jax.ShapeDtypeStruct((b, l, d_inner), BF16),
        scratch_shapes=[
            pltpu.VMEM((q + V7X_SUBLANES, cch), F32),
            pltpu.VMEM((n_pairs, SSD_STATE, 2 * SSD_HEAD_DIM), F32),
            pltpu.VMEM((q, d_inner), F32),
        ],
        compiler_params=pltpu.CompilerParams(
            dimension_semantics=("arbitrary", "arbitrary"), vmem_limit_bytes=40 * MIB),
        name="ssd",
    )(xbc, z, dtp, conv_w, conv_b, dt_bias, a_log, d_skip, norm_g)


def _diff_attn_kernel(slopes_ref, q_ref, k_ref, v_ref, lq1_ref, lk1_ref, lq2_ref, lk2_ref, sg_ref,
                      o_ref, m_ref, l_ref, acc_ref, *, tq, lam_init):
    d = DA_HEAD_DIM
    tk = tq
    h = pl.program_id(1)
    qi = pl.program_id(2)
    slope = slopes_ref[h]
    qv = q_ref[0]

    m_ref[...] = jnp.full(m_ref.shape, NEG_BIG, F32)
    l_ref[...] = jnp.zeros(l_ref.shape, F32)
    acc_ref[...] = jnp.zeros(acc_ref.shape, F32)

    lane = lax.broadcasted_iota(jnp.int32, (tk, 2 * d), 1)
    first = lane < d
    col_bias = slope * lax.broadcasted_iota(jnp.int32, (1, 2 * tk), 1).astype(F32)
    col_bias = jnp.where(lax.broadcasted_iota(jnp.int32, (1, 2 * tk), 1) < tk, col_bias, col_bias - slope * tk)

    def step(ki, masked):
        start = pl.multiple_of(ki * tk, tk)
        kv = k_ref[0, pl.ds(start, tk), :]
        vv = v_ref[0, pl.ds(start, tk), :]
        kz = jnp.concatenate([jnp.where(first, kv, jnp.zeros_like(kv)),
                              jnp.where(first, jnp.zeros_like(kv), kv)], axis=0)
        s = lax.dot_general(qv, kz, (((1,), (1,)), ((), ())), preferred_element_type=F32)
        t = s + col_bias
        if masked:
            r = lax.broadcasted_iota(jnp.int32, (tq, 2 * tk), 0)
            cc = lax.broadcasted_iota(jnp.int32, (tq, 2 * tk), 1)
            cc = jnp.where(cc < tk, cc, cc - tk)
            t = jnp.where(r >= cc, t, NEG_BIG)
        off = slope * (ki * tk).astype(F32)
        ps = []
        for mi in range(2):
            tm_ = t[:, mi * tk:(mi + 1) * tk]
            m_prev = m_ref[mi]
            m_new = jnp.maximum(m_prev, jnp.max(tm_, axis=1, keepdims=True) + off)
            alpha = jnp.exp(m_prev - m_new)
            pm = jnp.exp(tm_ - (m_new - off))
            l_ref[mi] = alpha * l_ref[mi] + jnp.sum(pm, axis=1, keepdims=True)
            m_ref[mi] = m_new
            acc_ref[mi] = alpha * acc_ref[mi]
            ps.append(pm.astype(BF16))
        pv = jnp.dot(jnp.concatenate(ps, axis=0), vv, preferred_element_type=F32)
        acc_ref[0] += pv[0:tq]
        acc_ref[1] += pv[tq:2 * tq]

    def body(ki, carry):
        step(ki, False)
        return carry

    lax.fori_loop(0, qi, body, 0)
    step(qi, True)

    lam = (jnp.exp(jnp.sum(lq1_ref[...] * lk1_ref[...], axis=1, keepdims=True))
           - jnp.exp(jnp.sum(lq2_ref[...] * lk2_ref[...], axis=1, keepdims=True)) + lam_init)
    out = acc_ref[0] / l_ref[0] - lam * (acc_ref[1] / l_ref[1])
    out = _rms(out) * sg_ref[...] * (1.0 - lam_init)
    o_ref[0] = out.astype(o_ref.dtype)


def _diff_attn(q, k, v, slopes, lq1, lk1, lq2, lk2, subln_g, *, tq, lam_init):
    b, l, w = q.shape
    hw = 2 * DA_HEAD_DIM
    n_heads = w // hw
    kern = functools.partial(_diff_attn_kernel, tq=tq, lam_init=lam_init)
    return pl.pallas_call(
        kern,
        grid=(b, n_heads, l // tq),
        in_specs=[
            pl.BlockSpec(memory_space=pltpu.SMEM),
            pl.BlockSpec((1, tq, hw), lambda i, j, s: (i, s, j)),
            pl.BlockSpec((1, l, hw), lambda i, j, s: (i, 0, j)),
            pl.BlockSpec((1, l, hw), lambda i, j, s: (i, 0, j)),
            _const_spec(lq1.shape), _const_spec(lk1.shape), _const_spec(lq2.shape), _const_spec(lk2.shape),
            _const_spec(subln_g.shape),
        ],
        out_specs=pl.BlockSpec((1, tq, hw), lambda i, j, s: (i, s, j)),
        out_shape=jax.ShapeDtypeStruct((b, l, w), BF16),
        scratch_shapes=[
            pltpu.VMEM((2, tq, 1), F32),
            pltpu.VMEM((2, tq, 1), F32),
            pltpu.VMEM((2, tq, hw), F32),
        ],
        compiler_params=pltpu.CompilerParams(
            dimension_semantics=("arbitrary", "arbitrary", "arbitrary"), vmem_limit_bytes=32 * MIB),
        name="diff_attn",
    )(slopes, q, k, v, lq1, lk1, lq2, lk2, subln_g)


def _merge_kernel(h_ref, ys_ref, ya_ref, gl_ref, gb_ref, ws_ref, wa_ref, wo_ref, pg_ref, o_ref):
    d = h_ref.shape[1]
    y_ssd = jnp.dot(ys_ref[...], ws_ref[...], preferred_element_type=F32)
    y_att = jnp.dot(ya_ref[...], wa_ref[...], preferred_element_type=F32)
    gates = jax.nn.sigmoid(gl_ref[...] + gb_ref[...])
    merged = (gates[:, 0:d] * y_ssd + gates[:, d:2 * d] * y_att).astype(BF16)
    mo = jnp.dot(merged, wo_ref[...], preferred_element_type=F32)
    o_ref[...] = h_ref[...] + _rms(mo) * pg_ref[...]


def _merge(h, ys, ya, gl, gate_b, w_ssd, w_da, w_out, post_g, *, tm):
    t, d = h.shape
    return pl.pallas_call(
        _merge_kernel,
        grid=(t // tm,),
        in_specs=[
            pl.BlockSpec((tm, d), lambda i: (i, 0)),
            pl.BlockSpec((tm, ys.shape[1]), lambda i: (i, 0)),
            pl.BlockSpec((tm, ya.shape[1]), lambda i: (i, 0)),
            pl.BlockSpec((tm, gl.shape[1]), lambda i: (i, 0)),
            _const_spec(gate_b.shape),
            _const_spec(w_ssd.shape),
            _const_spec(w_da.shape),
            _const_spec(w_out.shape),
            _const_spec(post_g.shape),
        ],
        out_specs=pl.BlockSpec((tm, d), lambda i: (i, 0)),
        out_shape=jax.ShapeDtypeStruct((t, d), F32),
        compiler_params=pltpu.CompilerParams(
            dimension_semantics=("arbitrary",), vmem_limit_bytes=48 * MIB),
        name="merge",
    )(h, ys, ya, gl, gate_b, w_ssd, w_da, w_out, post_g)


def kernel(x, ffn1_pre_g, ffn1_w_gate, ffn1_w_up, ffn1_w_down, ffn1_post_g, mix_pre_g, w_in, gate_b, ssd_conv_w, ssd_conv_b, ssd_dt_bias, ssd_A_log, ssd_D, ssd_norm_g, ssd_w_branch, da_lambda_q1, da_lambda_k1, da_lambda_q2, da_lambda_k2, da_subln_g, da_w_branch, w_out, mix_post_g, ffn2_pre_g, ffn2_w_gate, ffn2_w_up, ffn2_w_down, ffn2_post_g):
    b, l, d = x.shape
    depth = w_in.shape[0]
    d_inner = ssd_norm_g.shape[1]
    n_ssd_heads = ssd_D.shape[1]
    conv_ch = ssd_conv_w.shape[2]
    da_width = da_w_branch.shape[1]
    n_da_heads = da_width // (2 * DA_HEAD_DIM)
    sizes = [d_inner, conv_ch, n_ssd_heads, da_width, da_width, da_width, N_BRANCH * d]
    offs = [0]
    for s in sizes:
        offs.append(offs[-1] + s)
    slopes = jnp.asarray([2.0 ** (-8.0 * (i + 1) / n_da_heads) for i in range(n_da_heads)], F32)
    q_scale = 1.0 / math.sqrt(DA_HEAD_DIM)
    pad_h = V7X_LANES - n_ssd_heads

    def row(v):
        return v.reshape(1, -1).astype(F32)

    h = x.reshape(b * l, d)
    for i in range(depth):
        lam_init = 0.8 - 0.6 * math.exp(-0.3 * i)
        h = _ffn(h, row(ffn1_pre_g[i]), ffn1_w_gate[i].astype(BF16), ffn1_w_up[i].astype(BF16),
                 ffn1_w_down[i].astype(BF16), row(ffn1_post_g[i]), tm=512)

        wi = w_in[i].astype(BF16)
        ws = [wi[:, offs[j]:offs[j + 1]] for j in range(len(sizes))]
        ws[2] = jnp.pad(ws[2], ((0, 0), (0, pad_h)))
        z, xbc, dtp, qq, kk, vv, gl = _in_proj(h, row(mix_pre_g[i]), ws, tm=256, q_scale=q_scale)

        y_ssd = _ssd(xbc.reshape(b, l, conv_ch), z.reshape(b, l, d_inner), dtp.reshape(b, l, V7X_LANES),
                     ssd_conv_w[i].astype(F32), row(ssd_conv_b[i]),
                     jnp.pad(row(ssd_dt_bias[i]), ((0, 0), (0, pad_h))),
                     jnp.pad(row(ssd_A_log[i]), ((0, 0), (0, pad_h))),
                     row(jnp.repeat(ssd_D[i], SSD_HEAD_DIM)), row(ssd_norm_g[i]), n_heads=n_ssd_heads)

        y_att = _diff_attn(qq.reshape(b, l, da_width), kk.reshape(b, l, da_width), vv.reshape(b, l, da_width),
                           slopes, row(da_lambda_q1[i]), row(da_lambda_k1[i]), row(da_lambda_q2[i]),
                           row(da_lambda_k2[i]), row(da_subln_g[i]), tq=256, lam_init=lam_init)

        h = _merge(h, y_ssd.reshape(b * l, d_inner), y_att.reshape(b * l, da_width), gl, row(gate_b[i]),
                   ssd_w_branch[i].astype(BF16), da_w_branch[i].astype(BF16), w_out[i].astype(BF16),
                   row(mix_post_g[i]), tm=512)

        h = _ffn(h, row(ffn2_pre_g[i]), ffn2_w_gate[i].astype(BF16), ffn2_w_up[i].astype(BF16),
                 ffn2_w_down[i].astype(BF16), row(ffn2_post_g[i]), tm=512)
    return h.reshape(b, l, d)
```

```python
import functools
import math

import jax
import jax.numpy as jnp
from jax import lax
from jax.experimental import pallas as pl
from jax.experimental.pallas import tpu as pltpu

F32 = jnp.float32
BF16 = jnp.bfloat16

EPS = 1e-6

SSD_HEAD_DIM = 64
SSD_GROUPS = 4
SSD_STATE = 128
SSD_CHUNK = 128
DA_HEAD_DIM = 64
N_BRANCH = 2

V7X_LANES = 128
V7X_SUBLANES = 8
V7X_VMEM_BYTES = 64 * 1024 * 1024
MIB = 1024 * 1024

NEG_BIG = -1e30


def _const_spec(shape):
    nd = len(shape)
    return pl.BlockSpec(shape, lambda *_: (0,) * nd, pipeline_mode=pl.Buffered(1))


def _rms(x):
    return x * lax.rsqrt(jnp.mean(x * x, axis=-1, keepdims=True) + EPS)


def _silu(x):
    return x * jax.nn.sigmoid(x)


def _softplus(x):
    return jnp.maximum(x, 0.0) + jnp.log(1.0 + jnp.exp(-jnp.abs(x)))


def _ffn_kernel(x_ref, pre_g_ref, wg_ref, wu_ref, wd_ref, post_g_ref, o_ref):
    x = x_ref[...]
    xn = (_rms(x) * pre_g_ref[...]).astype(BF16)
    g = jnp.dot(xn, wg_ref[...], preferred_element_type=F32)
    u = jnp.dot(xn, wu_ref[...], preferred_element_type=F32)
    a = (_silu(g) * u).astype(BF16)
    y = jnp.dot(a, wd_ref[...], preferred_element_type=F32)
    o_ref[...] = x + 0.5 * (_rms(y) * post_g_ref[...])


def _ffn(x, pre_g, wg, wu, wd, post_g, *, tm):
    t, d = x.shape
    f = wg.shape[1]
    vmem = (3 * d * f * 2) + 4 * tm * d * 4 + 3 * tm * f * 4 + 8 * MIB
    return pl.pallas_call(
        _ffn_kernel,
        grid=(t // tm,),
        in_specs=[
            pl.BlockSpec((tm, d), lambda i: (i, 0)),
            _const_spec((1, d)),
            _const_spec((d, f)),
            _const_spec((d, f)),
            _const_spec((f, d)),
            _const_spec((1, d)),
        ],
        out_specs=pl.BlockSpec((tm, d), lambda i: (i, 0)),
        out_shape=jax.ShapeDtypeStruct((t, d), F32),
        compiler_params=pltpu.CompilerParams(
            dimension_semantics=("arbitrary",), vmem_limit_bytes=min(vmem, V7X_VMEM_BYTES - 4 * MIB)),
        name="ffn",
    )(x, pre_g, wg, wu, wd, post_g)


def _in_proj_kernel(h_ref, g_ref, wz_ref, wx_ref, wdt_ref, wq_ref, wk_ref, wv_ref, wgl_ref, cw_ref, cb_ref,
                    z_ref, xbc_ref, dt_ref, q_ref, k_ref, v_ref, gl_ref, xwin_ref, *, q_scale, tiles_per_seq):
    tm = h_ref.shape[0]
    conv_k = cw_ref.shape[0]
    halo = V7X_SUBLANES
    u = (_rms(h_ref[...]) * g_ref[...]).astype(BF16)

    def proj(w_ref):
        return jnp.dot(u, w_ref[...], preferred_element_type=F32)

    z_ref[...] = _silu(proj(wz_ref))

    @pl.when(pl.program_id(0) % tiles_per_seq == 0)
    def _():
        xwin_ref[0:halo, :] = jnp.zeros((halo, xwin_ref.shape[1]), F32)

    xw = proj(wx_ref)
    xwin_ref[halo:halo + tm, :] = xw
    acc = cb_ref[...] + cw_ref[conv_k - 1:conv_k, :] * xw
    for j in range(1, conv_k):
        acc = acc + cw_ref[conv_k - 1 - j:conv_k - j, :] * xwin_ref[halo - j:halo - j + tm, :]
    xwin_ref[0:halo, :] = xwin_ref[tm:tm + halo, :]
    xbc_ref[...] = _silu(acc)

    dt_ref[...] = proj(wdt_ref)
    q_ref[...] = (proj(wq_ref) * q_scale).astype(BF16)
    k_ref[...] = proj(wk_ref).astype(BF16)
    v_ref[...] = proj(wv_ref).astype(BF16)
    gl_ref[...] = proj(wgl_ref)


def _in_proj(h, g, ws, conv_w, conv_b, *, tm, q_scale, seq_len):
    t, d = h.shape
    widths = [w.shape[1] for w in ws]
    out_dtypes = [F32, F32, F32, BF16, BF16, BF16, F32]
    w_bytes = sum(d * n * 2 for n in widths)
    o_bytes = sum(tm * n * jnp.dtype(dt).itemsize for n, dt in zip(widths, out_dtypes))
    win_bytes = (tm + V7X_SUBLANES) * widths[1] * 4
    vmem = w_bytes + 2 * o_bytes + 2 * tm * d * 4 + 3 * tm * max(widths) * 4 + win_bytes + 8 * MIB
    return pl.pallas_call(
        functools.partial(_in_proj_kernel, q_scale=q_scale, tiles_per_seq=seq_len // tm),
        grid=(t // tm,),
        in_specs=[pl.BlockSpec((tm, d), lambda i: (i, 0)), _const_spec((1, d))]
        + [_const_spec(w.shape) for w in ws] + [_const_spec(conv_w.shape), _const_spec(conv_b.shape)],
        out_specs=[pl.BlockSpec((tm, n), lambda i: (i, 0)) for n in widths],
        out_shape=[jax.ShapeDtypeStruct((t, n), dt) for n, dt in zip(widths, out_dtypes)],
        scratch_shapes=[pltpu.VMEM((tm + V7X_SUBLANES, widths[1]), F32)],
        compiler_params=pltpu.CompilerParams(
            dimension_semantics=("arbitrary",), vmem_limit_bytes=min(vmem, V7X_VMEM_BYTES - 4 * MIB)),
        name="in_proj",
    )(h, g, *ws, conv_w, conv_b)


def _ssd_kernel(xc_ref, zs_ref, dt_ref, dtb_ref, alog_ref, dsk_ref, ng_ref,
                o_ref, state_ref, y_ref, *, n_heads, d_inner):
    q = SSD_CHUNK
    p = SSD_HEAD_DIM
    n = SSD_STATE
    heads_per_group = n_heads // SSD_GROUPS
    pairs_per_group = heads_per_group // 2
    group_w = d_inner // SSD_GROUPS
    c = pl.program_id(1)

    @pl.when(c == 0)
    def _():
        state_ref[...] = jnp.zeros(state_ref.shape, F32)

    dt = _softplus(dt_ref[0] + dtb_ref[...])
    a = dt * (-jnp.exp(alog_ref[...]))
    row = lax.broadcasted_iota(jnp.int32, (q, q), 0)
    col = lax.broadcasted_iota(jnp.int32, (q, q), 1)
    causal = row >= col
    tril = jnp.where(causal, 1.0, 0.0).astype(F32)
    acum = jnp.dot(tril, a, preferred_element_type=F32, precision=lax.Precision.HIGHEST)
    acum_t = acum.T
    dt_t = dt.T
    a_last_t = jnp.broadcast_to(acum_t[:, q - 1:q], acum_t.shape)
    w_t = jnp.exp(a_last_t - acum_t) * dt_t
    cdecay_t = jnp.exp(a_last_t)

    lane = lax.broadcasted_iota(jnp.int32, (q, 2 * p), 1)
    first = lane < p

    for g in range(SSD_GROUPS):
        b_g = xc_ref[0, :, d_inner + g * n:d_inner + (g + 1) * n]
        c_g = xc_ref[0, :, d_inner + (SSD_GROUPS + g) * n:d_inner + (SSD_GROUPS + g + 1) * n]
        bt_g = b_g.T
        cb = jnp.dot(c_g.astype(BF16), bt_g.astype(BF16), preferred_element_type=F32)
        for j in range(pairs_per_group):
            pair = g * pairs_per_group + j
            h0 = 2 * pair
            cols = slice(pair * 2 * p, (pair + 1) * 2 * p)
            x_pair = xc_ref[0, :, cols]
            m_parts, btw_parts, cs_parts = [], [], []
            for hh in (h0, h0 + 1):
                col_a = jnp.broadcast_to(acum[:, hh:hh + 1], (q, q))
                seg = col_a - acum_t[hh:hh + 1, :]
                lmat = jnp.exp(jnp.where(causal, seg, NEG_BIG))
                m_parts.append((cb * lmat * dt_t[hh:hh + 1, :]).astype(BF16))
                btw_parts.append((bt_g * w_t[hh:hh + 1, :]).astype(BF16))
                cs_parts.append((c_g * jnp.exp(col_a)).astype(BF16))
            xz = jnp.concatenate([jnp.where(first, x_pair, 0.0), jnp.where(first, 0.0, x_pair)],
                                 axis=0).astype(BF16)
            lhs = jnp.concatenate([jnp.concatenate(m_parts, axis=1),
                                   jnp.concatenate(btw_parts, axis=1)], axis=0)
            r1 = jnp.dot(lhs, xz, preferred_element_type=F32)
            prev = state_ref[pair]
            pz = jnp.concatenate([jnp.where(first, prev, 0.0), jnp.where(first, 0.0, prev)],
                                 axis=0).astype(BF16)
            y_off = jnp.dot(jnp.concatenate(cs_parts, axis=1), pz, preferred_element_type=F32)
            cd = jnp.where(first[0:1, :], cdecay_t[h0:h0 + 1, :], cdecay_t[h0 + 1:h0 + 2, :])
            state_ref[pair] = prev * cd + r1[q:q + n, :]
            y = r1[0:q, :] + y_off + dsk_ref[:, cols] * x_pair
            y_ref[:, cols] = y * zs_ref[0, :, cols]

    for g in range(SSD_GROUPS):
        gs = slice(g * group_w, (g + 1) * group_w)
        o_ref[0, :, gs] = (_rms(y_ref[:, gs]) * ng_ref[:, gs]).astype(o_ref.dtype)


def _ssd(xc, zs, dtp, dt_bias, a_log, d_skip, norm_g, *, n_heads):
    b, l, cch = xc.shape
    d_inner = zs.shape[-1]
    q = SSD_CHUNK
    assert SSD_STATE == q == 2 * SSD_HEAD_DIM == V7X_LANES
    n_pairs = n_heads // 2
    kern = functools.partial(_ssd_kernel, n_heads=n_heads, d_inner=d_inner)
    return pl.pallas_call(
        kern,
        grid=(b, l // q),
        in_specs=[
            pl.BlockSpec((1, q, cch), lambda i, j: (i, j, 0)),
            pl.BlockSpec((1, q, d_inner), lambda i, j: (i, j, 0)),
            pl.BlockSpec((1, q, V7X_LANES), lambda i, j: (i, j, 0)),
            _const_spec(dt_bias.shape),
            _const_spec(a_log.shape),
            _const_spec(d_skip.shape),
            _const_spec(norm_g.shape),
        ],
        out_specs=pl.BlockSpec((1, q, d_inner), lambda i, j: (i, j, 0)),
        out_shape=jax.ShapeDtypeStruct((b, l, d_inner), BF16),
        scratch_shapes=[
            pltpu.VMEM((n_pairs, SSD_STATE, 2 * SSD_HEAD_DIM), F32),
            pltpu.VMEM((q, d_inner), F32),
        ],
        compiler_params=pltpu.CompilerParams(
            dimension_semantics=("arbitrary", "arbitrary"), vmem_limit_bytes=32 * MIB),
        name="ssd",
    )(xc, zs, dtp, dt_bias, a_log, d_skip, norm_g)


def _diff_attn_kernel(slopes_ref, q_ref, k_ref, v_ref, lq1_ref, lk1_ref, lq2_ref, lk2_ref, sg_ref,
                      o_ref, *, blk, lam_init):
    d = DA_HEAD_DIM
    hw = 2 * d
    n_blk = q_ref.shape[1] // blk
    slope = slopes_ref[pl.program_id(1)]

    first = lax.broadcasted_iota(jnp.int32, (blk, hw), 1) < d
    bias = slope * lax.broadcasted_iota(jnp.int32, (1, blk), 1).astype(F32)
    rr = lax.broadcasted_iota(jnp.int32, (2 * blk, blk), 0)
    cc = lax.broadcasted_iota(jnp.int32, (2 * blk, blk), 1)
    tri = jnp.where(rr < blk, rr, rr - blk) >= cc
    ones = jnp.ones((blk, hw), BF16)
    lam = (jnp.exp(jnp.sum(lq1_ref[...] * lk1_ref[...], axis=1, keepdims=True))
           - jnp.exp(jnp.sum(lq2_ref[...] * lk2_ref[...], axis=1, keepdims=True)) + lam_init)

    for qi in range(n_blk):
        qv = q_ref[0, qi * blk:(qi + 1) * blk, :]
        zq = jnp.zeros_like(qv)
        qs = jnp.concatenate([jnp.where(first, qv, zq), jnp.where(first, zq, qv)], axis=0)
        m = None
        acc = None
        for ki in range(qi + 1):
            kv = k_ref[0, ki * blk:(ki + 1) * blk, :]
            vx = jnp.concatenate([v_ref[0, ki * blk:(ki + 1) * blk, :], ones], axis=1)
            s = lax.dot_general(qs, kv, (((1,), (1,)), ((), ())), preferred_element_type=F32)
            t = s + bias
            if ki == qi:
                t = jnp.where(tri, t, NEG_BIG)
            off = slope * float(ki * blk)
            bm = jnp.max(t, axis=1, keepdims=True) + off
            m_new = bm if m is None else jnp.maximum(m, bm)
            p = jnp.exp(t - (m_new - off)).astype(BF16)
            pv = jnp.dot(p, vx, preferred_element_type=F32)
            acc = pv if m is None else jnp.exp(m - m_new) * acc + pv
            m = m_new
        o1 = acc[0:blk, 0:hw] / acc[0:blk, hw:2 * hw]
        o2 = acc[blk:2 * blk, 0:hw] / acc[blk:2 * blk, hw:2 * hw]
        out = o1 - lam * o2
        out = _rms(out) * sg_ref[...] * (1.0 - lam_init)
        o_ref[0, qi * blk:(qi + 1) * blk, :] = out.astype(o_ref.dtype)


def _diff_attn(q, k, v, slopes, lq1, lk1, lq2, lk2, subln_g, *, blk, lam_init):
    b, l, w = q.shape
    hw = 2 * DA_HEAD_DIM
    n_heads = w // hw
    kern = functools.partial(_diff_attn_kernel, blk=blk, lam_init=lam_init)
    seq_spec = pl.BlockSpec((1, l, hw), lambda i, j: (i, 0, j))
    return pl.pallas_call(
        kern,
        grid=(b, n_heads),
        in_specs=[
            pl.BlockSpec(memory_space=pltpu.SMEM),
            seq_spec, seq_spec, seq_spec,
            _const_spec(lq1.shape), _const_spec(lk1.shape), _const_spec(lq2.shape), _const_spec(lk2.shape),
            _const_spec(subln_g.shape),
        ],
        out_specs=seq_spec,
        out_shape=jax.ShapeDtypeStruct((b, l, w), BF16),
        compiler_params=pltpu.CompilerParams(
            dimension_semantics=("arbitrary", "arbitrary"), vmem_limit_bytes=32 * MIB),
        name="diff_attn",
    )(slopes, q, k, v, lq1, lk1, lq2, lk2, subln_g)


def _merge_kernel(h_ref, ys_ref, ya_ref, gl_ref, gb_ref, ws_ref, wa_ref, wo_ref, pg_ref, o_ref):
    d = h_ref.shape[1]
    y_ssd = jnp.dot(ys_ref[...], ws_ref[...], preferred_element_type=F32)
    y_att = jnp.dot(ya_ref[...], wa_ref[...], preferred_element_type=F32)
    gates = jax.nn.sigmoid(gl_ref[...] + gb_ref[...])
    merged = (gates[:, 0:d] * y_ssd + gates[:, d:2 * d] * y_att).astype(BF16)
    mo = jnp.dot(merged, wo_ref[...], preferred_element_type=F32)
    o_ref[...] = h_ref[...] + _rms(mo) * pg_ref[...]


def _merge(h, ys, ya, gl, gate_b, w_ssd, w_da, w_out, post_g, *, tm):
    t, d = h.shape
    return pl.pallas_call(
        _merge_kernel,
        grid=(t // tm,),
        in_specs=[
            pl.BlockSpec((tm, d), lambda i: (i, 0)),
            pl.BlockSpec((tm, ys.shape[1]), lambda i: (i, 0)),
            pl.BlockSpec((tm, ya.shape[1]), lambda i: (i, 0)),
            pl.BlockSpec((tm, gl.shape[1]), lambda i: (i, 0)),
            _const_spec(gate_b.shape),
            _const_spec(w_ssd.shape),
            _const_spec(w_da.shape),
            _const_spec(w_out.shape),
            _const_spec(post_g.shape),
        ],
        out_specs=pl.BlockSpec((tm, d), lambda i: (i, 0)),
        out_shape=jax.ShapeDtypeStruct((t, d), F32),
        compiler_params=pltpu.CompilerParams(
            dimension_semantics=("arbitrary",), vmem_limit_bytes=48 * MIB),
        name="merge",
    )(h, ys, ya, gl, gate_b, w_ssd, w_da, w_out, post_g)


def kernel(x, ffn1_pre_g, ffn1_w_gate, ffn1_w_up, ffn1_w_down, ffn1_post_g, mix_pre_g, w_in, gate_b, ssd_conv_w, ssd_conv_b, ssd_dt_bias, ssd_A_log, ssd_D, ssd_norm_g, ssd_w_branch, da_lambda_q1, da_lambda_k1, da_lambda_q2, da_lambda_k2, da_subln_g, da_w_branch, w_out, mix_post_g, ffn2_pre_g, ffn2_w_gate, ffn2_w_up, ffn2_w_down, ffn2_post_g):
    b, l, d = x.shape
    depth = w_in.shape[0]
    d_inner = ssd_norm_g.shape[1]
    n_ssd_heads = ssd_D.shape[1]
    conv_ch = ssd_conv_w.shape[2]
    da_width = da_w_branch.shape[1]
    n_da_heads = da_width // (2 * DA_HEAD_DIM)
    sizes = [d_inner, conv_ch, n_ssd_heads, da_width, da_width, da_width, N_BRANCH * d]
    offs = [0]
    for s in sizes:
        offs.append(offs[-1] + s)
    slopes = jnp.asarray([2.0 ** (-8.0 * (i + 1) / n_da_heads) for i in range(n_da_heads)], F32)
    q_scale = 1.0 / math.sqrt(DA_HEAD_DIM)
    pad_h = V7X_LANES - n_ssd_heads

    def row(v):
        return v.reshape(1, -1).astype(F32)

    h = x.reshape(b * l, d)
    for i in range(depth):
        lam_init = 0.8 - 0.6 * math.exp(-0.3 * i)
        h = _ffn(h, row(ffn1_pre_g[i]), ffn1_w_gate[i].astype(BF16), ffn1_w_up[i].astype(BF16),
                 ffn1_w_down[i].astype(BF16), row(ffn1_post_g[i]), tm=512)

        wi = w_in[i].astype(BF16)
        ws = [wi[:, offs[j]:offs[j + 1]] for j in range(len(sizes))]
        ws[2] = jnp.pad(ws[2], ((0, 0), (0, pad_h)))
        zs, xc, dtp, qq, kk, vv, gl = _in_proj(h, row(mix_pre_g[i]), ws, ssd_conv_w[i].astype(F32),
                                               row(ssd_conv_b[i]), tm=256, q_scale=q_scale, seq_len=l)

        y_ssd = _ssd(xc.reshape(b, l, conv_ch), zs.reshape(b, l, d_inner), dtp.reshape(b, l, V7X_LANES),
                     jnp.pad(row(ssd_dt_bias[i]), ((0, 0), (0, pad_h))),
                     jnp.pad(row(ssd_A_log[i]), ((0, 0), (0, pad_h))),
                     row(jnp.repeat(ssd_D[i], SSD_HEAD_DIM)), row(ssd_norm_g[i]), n_heads=n_ssd_heads)

        y_att = _diff_attn(qq.reshape(b, l, da_width), kk.reshape(b, l, da_width), vv.reshape(b, l, da_width),
                           slopes, row(da_lambda_q1[i]), row(da_lambda_k1[i]), row(da_lambda_q2[i]),
                           row(da_lambda_k2[i]), row(da_subln_g[i]), blk=256, lam_init=lam_init)

        h = _merge(h, y_ssd.reshape(b * l, d_inner), y_att.reshape(b * l, da_width), gl, row(gate_b[i]),
                   ssd_w_branch[i].astype(BF16), da_w_branch[i].astype(BF16), w_out[i].astype(BF16),
                   row(mix_post_g[i]), tm=512)

        h = _ffn(h, row(ffn2_pre_g[i]), ffn2_w_gate[i].astype(BF16), ffn2_w_up[i].astype(BF16),
                 ffn2_w_down[i].astype(BF16), row(ffn2_post_g[i]), tm=512)
    return h.reshape(b, l, d)
```

```python
import functools
import math

import jax
import jax.numpy as jnp
from jax import lax
from jax.experimental import pallas as pl
from jax.experimental.pallas import tpu as pltpu

F32 = jnp.float32
BF16 = jnp.bfloat16

EPS = 1e-6
LOG2E = 1.4426950408889634

SSD_HEAD_DIM = 64
SSD_GROUPS = 4
SSD_STATE = 128
SSD_CHUNK = 128
DA_HEAD_DIM = 64
N_BRANCH = 2

V7X_LANES = 128
V7X_SUBLANES = 8
V7X_VMEM_BYTES = 64 * 1024 * 1024
MIB = 1024 * 1024

NEG_BIG = -1e30
N_BIAS_TERMS = 3


def _const_spec(shape):
    nd = len(shape)
    return pl.BlockSpec(shape, lambda *_: (0,) * nd, pipeline_mode=pl.Buffered(1))


def _rms(x):
    return x * lax.rsqrt(jnp.mean(x * x, axis=-1, keepdims=True) + EPS)


def _silu(x):
    return x * jax.nn.sigmoid(x)


def _softplus(x):
    return jnp.maximum(x, 0.0) + jnp.log(1.0 + jnp.exp(-jnp.abs(x)))


def _ffn_kernel(x_ref, pre_g_ref, wg_ref, wu_ref, wd_ref, post_g_ref, o_ref):
    x = x_ref[...]
    xn = (_rms(x) * pre_g_ref[...]).astype(BF16)
    g = jnp.dot(xn, wg_ref[...], preferred_element_type=F32)
    u = jnp.dot(xn, wu_ref[...], preferred_element_type=F32)
    a = (_silu(g) * u).astype(BF16)
    y = jnp.dot(a, wd_ref[...], preferred_element_type=F32)
    o_ref[...] = x + 0.5 * (_rms(y) * post_g_ref[...])


def _ffn(x, pre_g, wg, wu, wd, post_g, *, tm):
    t, d = x.shape
    f = wg.shape[1]
    vmem = (3 * d * f * 2) + 4 * tm * d * 4 + 3 * tm * f * 4 + 8 * MIB
    return pl.pallas_call(
        _ffn_kernel,
        grid=(t // tm,),
        in_specs=[
            pl.BlockSpec((tm, d), lambda i: (i, 0)),
            _const_spec((1, d)),
            _const_spec((d, f)),
            _const_spec((d, f)),
            _const_spec((f, d)),
            _const_spec((1, d)),
        ],
        out_specs=pl.BlockSpec((tm, d), lambda i: (i, 0)),
        out_shape=jax.ShapeDtypeStruct((t, d), F32),
        compiler_params=pltpu.CompilerParams(
            dimension_semantics=("arbitrary",), vmem_limit_bytes=min(vmem, V7X_VMEM_BYTES - 4 * MIB)),
        name="ffn",
    )(x, pre_g, wg, wu, wd, post_g)


def _in_proj_kernel(h_ref, g_ref, wz_ref, wx_ref, wdt_ref, wq_ref, wk_ref, wv_ref, wgl_ref,
                    z_ref, xbc_ref, dt_ref, q_ref, k_ref, v_ref, gl_ref, *, q_scale):
    u = (_rms(h_ref[...]) * g_ref[...]).astype(BF16)

    def proj(w_ref):
        return jnp.dot(u, w_ref[...], preferred_element_type=F32)

    z_ref[...] = proj(wz_ref)
    xbc_ref[...] = proj(wx_ref)
    dt_ref[...] = proj(wdt_ref)
    q_ref[...] = (proj(wq_ref) * q_scale).astype(BF16)
    k_ref[...] = proj(wk_ref).astype(BF16)
    v_ref[...] = proj(wv_ref).astype(BF16)
    gl_ref[...] = proj(wgl_ref)


def _in_proj(h, g, ws, *, tm, q_scale):
    t, d = h.shape
    widths = [w.shape[1] for w in ws]
    out_dtypes = [F32, F32, F32, BF16, BF16, BF16, F32]
    w_bytes = sum(d * n * 2 for n in widths)
    o_bytes = sum(tm * n * jnp.dtype(dt).itemsize for n, dt in zip(widths, out_dtypes))
    vmem = w_bytes + 2 * o_bytes + 2 * tm * d * 4 + tm * max(widths) * 4 + 8 * MIB
    return pl.pallas_call(
        functools.partial(_in_proj_kernel, q_scale=q_scale),
        grid=(t // tm,),
        in_specs=[pl.BlockSpec((tm, d), lambda i: (i, 0)), _const_spec((1, d))]
        + [_const_spec(w.shape) for w in ws],
        out_specs=[pl.BlockSpec((tm, n), lambda i: (i, 0)) for n in widths],
        out_shape=[jax.ShapeDtypeStruct((t, n), dt) for n, dt in zip(widths, out_dtypes)],
        compiler_params=pltpu.CompilerParams(
            dimension_semantics=("arbitrary",), vmem_limit_bytes=min(vmem, V7X_VMEM_BYTES - 4 * MIB)),
        name="in_proj",
    )(h, g, *ws)


def _ssd_kernel(xbc_ref, z_ref, dt_ref, cw_ref, cb_ref, dtb_ref, alog_ref, dsk_ref, ng_ref,
                o_ref, xwin_ref, xc_ref, state_ref, y_ref, *, n_heads, d_inner, conv_k):
    q = SSD_CHUNK
    p = SSD_HEAD_DIM
    n = SSD_STATE
    halo = V7X_SUBLANES
    heads_per_group = n_heads // SSD_GROUPS
    pairs_per_group = heads_per_group // 2
    group_w = d_inner // SSD_GROUPS
    c = pl.program_id(1)

    @pl.when(c == 0)
    def _():
        xwin_ref[0:halo, :] = jnp.zeros((halo, xwin_ref.shape[1]), F32)
        state_ref[...] = jnp.zeros(state_ref.shape, F32)

    xwin_ref[halo:halo + q, :] = xbc_ref[0]
    acc = cb_ref[...] + cw_ref[conv_k - 1:conv_k, :] * xbc_ref[0]
    for j in range(1, conv_k):
        acc = acc + cw_ref[conv_k - 1 - j:conv_k - j, :] * xwin_ref[halo - j:halo - j + q, :]
    xwin_ref[0:halo, :] = xwin_ref[q:q + halo, :]
    xc_ref[...] = _silu(acc)

    dt = _softplus(dt_ref[0] + dtb_ref[...])
    a = dt * (-jnp.exp(alog_ref[...]))
    row = lax.broadcasted_iota(jnp.int32, (q, q), 0)
    col = lax.broadcasted_iota(jnp.int32, (q, q), 1)
    causal = row >= col
    tril = jnp.where(causal, 1.0, 0.0).astype(F32)
    acum2 = jnp.dot(tril, a, preferred_element_type=F32, precision=lax.Precision.HIGHEST) * LOG2E
    acum2_t = acum2.T
    dt_t = dt.T
    row2_t = acum2_t - jnp.log2(dt_t)
    a_last2_t = jnp.broadcast_to(acum2_t[:, q - 1:q], acum2_t.shape)
    w_t = jnp.exp2(a_last2_t - acum2_t) * dt_t
    cdecay_t = jnp.exp2(a_last2_t)

    first = lax.broadcasted_iota(jnp.int32, (q, 2 * p), 1) < p

    for g in range(SSD_GROUPS):
        b_g = xc_ref[:, d_inner + g * n:d_inner + (g + 1) * n]
        c_g = xc_ref[:, d_inner + (SSD_GROUPS + g) * n:d_inner + (SSD_GROUPS + g + 1) * n]
        bt_g = b_g.T
        c_bf = c_g.astype(BF16)
        cb = jnp.dot(c_bf, bt_g.astype(BF16), preferred_element_type=F32)
        for j in range(pairs_per_group):
            pair = g * pairs_per_group + j
            h0 = 2 * pair
            cols = slice(pair * 2 * p, (pair + 1) * 2 * p)
            x_pair = xc_ref[:, cols]
            m_parts, btw_parts, ea_parts = [], [], []
            for hh in (h0, h0 + 1):
                col_a2 = jnp.broadcast_to(acum2[:, hh:hh + 1], (q, q))
                lmat = jnp.exp2(jnp.where(causal, col_a2 - row2_t[hh:hh + 1, :], NEG_BIG))
                m_parts.append((cb * lmat).astype(BF16))
                btw_parts.append((bt_g * w_t[hh:hh + 1, :]).astype(BF16))
                ea_parts.append(jnp.exp2(col_a2))
            xb = x_pair.astype(BF16)
            zb = jnp.zeros_like(xb)
            xz = jnp.concatenate([jnp.where(first, xb, zb), jnp.where(first, zb, xb)], axis=0)
            lhs = jnp.concatenate([jnp.concatenate(m_parts, axis=1),
                                   jnp.concatenate(btw_parts, axis=1)], axis=0)
            r1 = jnp.dot(lhs, xz, preferred_element_type=F32)
            prev = state_ref[pair]
            y_off = (jnp.dot(c_bf, prev.astype(BF16), preferred_element_type=F32)
                     * jnp.where(first, ea_parts[0], ea_parts[1]))
            cd = jnp.where(first[0:1, :], cdecay_t[h0:h0 + 1, :], cdecay_t[h0 + 1:h0 + 2, :])
            state_ref[pair] = prev * cd + r1[q:q + n, :]
            y = r1[0:q, :] + y_off + dsk_ref[:, cols] * x_pair
            y_ref[:, cols] = y * _silu(z_ref[0, :, cols])

    for g in range(SSD_GROUPS):
        gs = slice(g * group_w, (g + 1) * group_w)
        o_ref[0, :, gs] = (_rms(y_ref[:, gs]) * ng_ref[:, gs]).astype(o_ref.dtype)


def _ssd(xbc, z, dtp, conv_w, conv_b, dt_bias, a_log, d_skip, norm_g, *, n_heads):
    b, l, cch = xbc.shape
    d_inner = z.shape[-1]
    q = SSD_CHUNK
    assert SSD_STATE == q == 2 * SSD_HEAD_DIM == V7X_LANES
    conv_k = conv_w.shape[0]
    n_pairs = n_heads // 2
    kern = functools.partial(_ssd_kernel, n_heads=n_heads, d_inner=d_inner, conv_k=conv_k)
    return pl.pallas_call(
        kern,
        grid=(b, l // q),
        in_specs=[
            pl.BlockSpec((1, q, cch), lambda i, j: (i, j, 0)),
            pl.BlockSpec((1, q, d_inner), lambda i, j: (i, j, 0)),
            pl.BlockSpec((1, q, V7X_LANES), lambda i, j: (i, j, 0)),
            _const_spec(conv_w.shape),
            _const_spec(conv_b.shape),
            _const_spec(dt_bias.shape),
            _const_spec(a_log.shape),
            _const_spec(d_skip.shape),
            _const_spec(norm_g.shape),
        ],
        out_specs=pl.BlockSpec((1, q, d_inner), lambda i, j: (i, j, 0)),
        out_shape=jax.ShapeDtypeStruct((b, l, d_inner), BF16),
        scratch_shapes=[
            pltpu.VMEM((q + V7X_SUBLANES, cch), F32),
            pltpu.VMEM((q, cch), F32),
            pltpu.VMEM((n_pairs, SSD_STATE, 2 * SSD_HEAD_DIM), F32),
            pltpu.VMEM((q, d_inner), F32),
        ],
        compiler_params=pltpu.CompilerParams(
            dimension_semantics=("arbitrary", "arbitrary"), vmem_limit_bytes=40 * MIB),
        name="ssd",
    )(xbc, z, dtp, conv_w, conv_b, dt_bias, a_log, d_skip, norm_g)


def _diff_attn_kernel(slopes_ref, q_ref, k_ref, v_ref, lq1_ref, lk1_ref, lq2_ref, lk2_ref, sg_ref,
                      o_ref, *, blk, lam_init):
    d = DA_HEAD_DIM
    hw = 2 * d
    n_blk = q_ref.shape[1] // blk
    slope2 = slopes_ref[pl.program_id(1)] * LOG2E
    blk_shift = slope2 * float(blk)

    first = lax.broadcasted_iota(jnp.int32, (blk, hw), 1) < d
    lane = lax.broadcasted_iota(jnp.int32, (blk, hw), 1)
    rest = slope2 * lax.broadcasted_iota(jnp.int32, (blk, hw), 0).astype(F32)
    k_bias = jnp.zeros((blk, hw), F32)
    for i in range(N_BIAS_TERMS):
        piece = rest.astype(BF16).astype(F32)
        k_bias = jnp.where(lane == i, piece, k_bias)
        rest = rest - piece
    k_bias = k_bias.astype(BF16)
    q_ones = jnp.where(lax.broadcasted_iota(jnp.int32, (2 * blk, hw), 1) < N_BIAS_TERMS, 1.0, 0.0).astype(BF16)
    rr = lax.broadcasted_iota(jnp.int32, (2 * blk, blk), 0)
    cc = lax.broadcasted_iota(jnp.int32, (2 * blk, blk), 1)
    tri = jnp.where(rr < blk, rr, rr - blk) >= cc
    ones = jnp.ones((blk, hw), BF16)
    lam = (jnp.exp(jnp.sum(lq1_ref[...] * lk1_ref[...], axis=1, keepdims=True))
           - jnp.exp(jnp.sum(lq2_ref[...] * lk2_ref[...], axis=1, keepdims=True)) + lam_init)

    for qi in range(n_blk):
        qv = q_ref[0, qi * blk:(qi + 1) * blk, :]
        zq = jnp.zeros_like(qv)
        qs = jnp.concatenate([jnp.where(first, qv, zq), jnp.where(first, zq, qv)], axis=0)
        qx = jnp.concatenate([qs, q_ones], axis=1)
        m = None
        acc = None
        for ki in range(qi + 1):
            kx = jnp.concatenate([k_ref[0, ki * blk:(ki + 1) * blk, :], k_bias], axis=1)
            vx = jnp.concatenate([v_ref[0, ki * blk:(ki + 1) * blk, :], ones], axis=1)
            t = lax.dot_general(qx, kx, (((1,), (1,)), ((), ())), preferred_element_type=F32)
            if ki == qi:
                t = jnp.where(tri, t, NEG_BIG)
            bm = jnp.max(t, axis=1, keepdims=True)
            if m is None:
                m_new = bm
            else:
                m_old = m - blk_shift
                m_new = jnp.maximum(m_old, bm)
            p = jnp.exp2(t - m_new).astype(BF16)
            pv = jnp.dot(p, vx, preferred_element_type=F32)
            acc = pv if m is None else jnp.exp2(m_old - m_new) * acc + pv
            m = m_new
        o1 = acc[0:blk, 0:hw] / acc[0:blk, hw:2 * hw]
        o2 = acc[blk:2 * blk, 0:hw] / acc[blk:2 * blk, hw:2 * hw]
        out = o1 - lam * o2
        out = _rms(out) * sg_ref[...] * (1.0 - lam_init)
        o_ref[0, qi * blk:(qi + 1) * blk, :] = out.astype(o_ref.dtype)


def _diff_attn(q, k, v, slopes, lq1, lk1, lq2, lk2, subln_g, *, blk, lam_init):
    b, l, w = q.shape
    hw = 2 * DA_HEAD_DIM
    n_heads = w // hw
    kern = functools.partial(_diff_attn_kernel, blk=blk, lam_init=lam_init)
    seq_spec = pl.BlockSpec((1, l, hw), lambda i, j: (i, 0, j))
    return pl.pallas_call(
        kern,
        grid=(b, n_heads),
        in_specs=[
            pl.BlockSpec(memory_space=pltpu.SMEM),
            seq_spec, seq_spec, seq_spec,
            _const_spec(lq1.shape), _const_spec(lk1.shape), _const_spec(lq2.shape), _const_spec(lk2.shape),
            _const_spec(subln_g.shape),
        ],
        out_specs=seq_spec,
        out_shape=jax.ShapeDtypeStruct((b, l, w), BF16),
        compiler_params=pltpu.CompilerParams(
            dimension_semantics=("arbitrary", "arbitrary"), vmem_limit_bytes=32 * MIB),
        name="diff_attn",
    )(slopes, q, k, v, lq1, lk1, lq2, lk2, subln_g)


def _merge_kernel(h_ref, ys_ref, ya_ref, gl_ref, gb_ref, ws_ref, wa_ref, wo_ref, pg_ref, o_ref):
    d = h_ref.shape[1]
    y_ssd = jnp.dot(ys_ref[...], ws_ref[...], preferred_element_type=F32)
    y_att = jnp.dot(ya_ref[...], wa_ref[...], preferred_element_type=F32)
    gates = jax.nn.sigmoid(gl_ref[...] + gb_ref[...])
    merged = (gates[:, 0:d] * y_ssd + gates[:, d:2 * d] * y_att).astype(BF16)
    mo = jnp.dot(merged, wo_ref[...], preferred_element_type=F32)
    o_ref[...] = h_ref[...] + _rms(mo) * pg_ref[...]


def _merge(h, ys, ya, gl, gate_b, w_ssd, w_da, w_out, post_g, *, tm):
    t, d = h.shape
    return pl.pallas_call(
        _merge_kernel,
        grid=(t // tm,),
        in_specs=[
            pl.BlockSpec((tm, d), lambda i: (i, 0)),
            pl.BlockSpec((tm, ys.shape[1]), lambda i: (i, 0)),
            pl.BlockSpec((tm, ya.shape[1]), lambda i: (i, 0)),
            pl.BlockSpec((tm, gl.shape[1]), lambda i: (i, 0)),
            _const_spec(gate_b.shape),
            _const_spec(w_ssd.shape),
            _const_spec(w_da.shape),
            _const_spec(w_out.shape),
            _const_spec(post_g.shape),
        ],
        out_specs=pl.BlockSpec((tm, d), lambda i: (i, 0)),
        out_shape=jax.ShapeDtypeStruct((t, d), F32),
        compiler_params=pltpu.CompilerParams(
            dimension_semantics=("arbitrary",), vmem_limit_bytes=48 * MIB),
        name="merge",
    )(h, ys, ya, gl, gate_b, w_ssd, w_da, w_out, post_g)


def kernel(x, ffn1_pre_g, ffn1_w_gate, ffn1_w_up, ffn1_w_down, ffn1_post_g, mix_pre_g, w_in, gate_b, ssd_conv_w, ssd_conv_b, ssd_dt_bias, ssd_A_log, ssd_D, ssd_norm_g, ssd_w_branch, da_lambda_q1, da_lambda_k1, da_lambda_q2, da_lambda_k2, da_subln_g, da_w_branch, w_out, mix_post_g, ffn2_pre_g, ffn2_w_gate, ffn2_w_up, ffn2_w_down, ffn2_post_g):
    b, l, d = x.shape
    depth = w_in.shape[0]
    d_inner = ssd_norm_g.shape[1]
    n_ssd_heads = ssd_D.shape[1]
    conv_ch = ssd_conv_w.shape[2]
    da_width = da_w_branch.shape[1]
    n_da_heads = da_width // (2 * DA_HEAD_DIM)
    sizes = [d_inner, conv_ch, n_ssd_heads, da_width, da_width, da_width, N_BRANCH * d]
    offs = [0]
    for s in sizes:
        offs.append(offs[-1] + s)
    slopes = jnp.asarray([2.0 ** (-8.0 * (i + 1) / n_da_heads) for i in range(n_da_heads)], F32)
    q_scale = LOG2E / math.sqrt(DA_HEAD_DIM)
    pad_h = V7X_LANES - n_ssd_heads

    def row(v):
        return v.reshape(1, -1).astype(F32)

    h = x.reshape(b * l, d)
    for i in range(depth):
        lam_init = 0.8 - 0.6 * math.exp(-0.3 * i)
        h = _ffn(h, row(ffn1_pre_g[i]), ffn1_w_gate[i].astype(BF16), ffn1_w_up[i].astype(BF16),
                 ffn1_w_down[i].astype(BF16), row(ffn1_post_g[i]), tm=512)

        wi = w_in[i].astype(BF16)
        ws = [wi[:, offs[j]:offs[j + 1]] for j in range(len(sizes))]
        ws[2] = jnp.pad(ws[2], ((0, 0), (0, pad_h)))
        z, xbc, dtp, qq, kk, vv, gl = _in_proj(h, row(mix_pre_g[i]), ws, tm=256, q_scale=q_scale)

        y_ssd = _ssd(xbc.reshape(b, l, conv_ch), z.reshape(b, l, d_inner), dtp.reshape(b, l, V7X_LANES),
                     ssd_conv_w[i].astype(F32), row(ssd_conv_b[i]),
                     jnp.pad(row(ssd_dt_bias[i]), ((0, 0), (0, pad_h))),
                     jnp.pad(row(ssd_A_log[i]), ((0, 0), (0, pad_h))),
                     row(jnp.repeat(ssd_D[i], SSD_HEAD_DIM)), row(ssd_norm_g[i]), n_heads=n_ssd_heads)

        y_att = _diff_attn(qq.reshape(b, l, da_width), kk.reshape(b, l, da_width), vv.reshape(b, l, da_width),
                           slopes, row(da_lambda_q1[i]), row(da_lambda_k1[i]), row(da_lambda_q2[i]),
                           row(da_lambda_k2[i]), row(da_subln_g[i]), blk=256, lam_init=lam_init)

        h = _merge(h, y_ssd.reshape(b * l, d_inner), y_att.reshape(b * l, da_width), gl, row(gate_b[i]),
                   ssd_w_branch[i].astype(BF16), da_w_branch[i].astype(BF16), w_out[i].astype(BF16),
                   row(mix_post_g[i]), tm=512)

        h = _ffn(h, row(ffn2_pre_g[i]), ffn2_w_gate[i].astype(BF16), ffn2_w_up[i].astype(BF16),
                 ffn2_w_down[i].astype(BF16), row(ffn2_post_g[i]), tm=512)
    return h.reshape(b, l, d)
```

```python
import functools
import math

import jax
import jax.numpy as jnp
from jax import lax
from jax.experimental import pallas as pl
from jax.experimental.pallas import tpu as pltpu

F32 = jnp.float32
BF16 = jnp.bfloat16

EPS = 1e-6
LOG2E = 1.4426950408889634

SSD_HEAD_DIM = 64
SSD_GROUPS = 4
SSD_STATE = 128
SSD_CHUNK = 128
DA_HEAD_DIM = 64
N_BRANCH = 2

V7X_LANES = 128
V7X_SUBLANES = 8
V7X_VMEM_BYTES = 64 * 1024 * 1024
MIB = 1024 * 1024

NEG_BIG = -1e30
KV_BLOCKS_PER_STEP = 1
N_BIAS_TERMS = 3


def _const_spec(shape):
    nd = len(shape)
    return pl.BlockSpec(shape, lambda *_: (0,) * nd, pipeline_mode=pl.Buffered(1))


def _rms(x):
    return x * lax.rsqrt(jnp.mean(x * x, axis=-1, keepdims=True) + EPS)


def _silu(x):
    hx = 0.5 * x
    return hx + hx * jnp.tanh(hx)


def _softplus(x):
    return jnp.maximum(x, 0.0) + jnp.log(1.0 + jnp.exp(-jnp.abs(x)))


def _ffn_kernel(x_ref, pre_g_ref, wg_ref, wu_ref, wd_ref, post_g_ref, o_ref):
    x = x_ref[...]
    xn = (_rms(x) * pre_g_ref[...]).astype(BF16)
    g = jnp.dot(xn, wg_ref[...], preferred_element_type=F32)
    u = jnp.dot(xn, wu_ref[...], preferred_element_type=F32)
    a = (_silu(g) * u).astype(BF16)
    y = jnp.dot(a, wd_ref[...], preferred_element_type=F32)
    o_ref[...] = x + 0.5 * (_rms(y) * post_g_ref[...])


def _ffn(x, pre_g, wg, wu, wd, post_g, *, tm):
    t, d = x.shape
    f = wg.shape[1]
    vmem = (3 * d * f * 2) + 4 * tm * d * 4 + 3 * tm * f * 4 + 8 * MIB
    return pl.pallas_call(
        _ffn_kernel,
        grid=(t // tm,),
        in_specs=[
            pl.BlockSpec((tm, d), lambda i: (i, 0)),
            _const_spec((1, d)),
            _const_spec((d, f)),
            _const_spec((d, f)),
            _const_spec((f, d)),
            _const_spec((1, d)),
        ],
        out_specs=pl.BlockSpec((tm, d), lambda i: (i, 0)),
        out_shape=jax.ShapeDtypeStruct((t, d), F32),
        compiler_params=pltpu.CompilerParams(
            dimension_semantics=("arbitrary",), vmem_limit_bytes=min(vmem, V7X_VMEM_BYTES - 4 * MIB)),
        name="ffn",
    )(x, pre_g, wg, wu, wd, post_g)


def _in_proj_kernel(h_ref, g_ref, wz_ref, wx_ref, wdt_ref, wq_ref, wk_ref, wv_ref, wgl_ref,
                    z_ref, xbc_ref, dt_ref, q_ref, k_ref, v_ref, gl_ref, *, q_scale):
    u = (_rms(h_ref[...]) * g_ref[...]).astype(BF16)

    def proj(w_ref):
        return jnp.dot(u, w_ref[...], preferred_element_type=F32)

    z_ref[...] = proj(wz_ref)
    xbc_ref[...] = proj(wx_ref)
    dt_ref[...] = proj(wdt_ref)
    q_ref[...] = (proj(wq_ref) * q_scale).astype(BF16)
    k_ref[...] = proj(wk_ref).astype(BF16)
    v_ref[...] = proj(wv_ref).astype(BF16)
    gl_ref[...] = proj(wgl_ref)


def _in_proj(h, g, ws, *, tm, q_scale):
    t, d = h.shape
    widths = [w.shape[1] for w in ws]
    out_dtypes = [F32, F32, F32, BF16, BF16, BF16, F32]
    w_bytes = sum(d * n * 2 for n in widths)
    o_bytes = sum(tm * n * jnp.dtype(dt).itemsize for n, dt in zip(widths, out_dtypes))
    vmem = w_bytes + 2 * o_bytes + 2 * tm * d * 4 + tm * max(widths) * 4 + 8 * MIB
    return pl.pallas_call(
        functools.partial(_in_proj_kernel, q_scale=q_scale),
        grid=(t // tm,),
        in_specs=[pl.BlockSpec((tm, d), lambda i: (i, 0)), _const_spec((1, d))]
        + [_const_spec(w.shape) for w in ws],
        out_specs=[pl.BlockSpec((tm, n), lambda i: (i, 0)) for n in widths],
        out_shape=[jax.ShapeDtypeStruct((t, n), dt) for n, dt in zip(widths, out_dtypes)],
        compiler_params=pltpu.CompilerParams(
            dimension_semantics=("arbitrary",), vmem_limit_bytes=min(vmem, V7X_VMEM_BYTES - 4 * MIB)),
        name="in_proj",
    )(h, g, *ws)


def _ssd_kernel(xbc_ref, z_ref, dt_ref, cw_ref, cb_ref, dtb_ref, alog_ref, dsk_ref, ng_ref,
                o_ref, xwin_ref, xc_ref, state_ref, y_ref, *, n_heads, d_inner, conv_k):
    q = SSD_CHUNK
    p = SSD_HEAD_DIM
    n = SSD_STATE
    halo = V7X_SUBLANES
    heads_per_group = n_heads // SSD_GROUPS
    pairs_per_group = heads_per_group // 2
    group_w = d_inner // SSD_GROUPS
    c = pl.program_id(1)

    @pl.when(c == 0)
    def _():
        xwin_ref[0:halo, :] = jnp.zeros((halo, xwin_ref.shape[1]), F32)
        state_ref[...] = jnp.zeros(state_ref.shape, F32)

    xwin_ref[halo:halo + q, :] = xbc_ref[0]
    acc = cb_ref[...] + cw_ref[conv_k - 1:conv_k, :] * xbc_ref[0]
    for j in range(1, conv_k):
        acc = acc + cw_ref[conv_k - 1 - j:conv_k - j, :] * xwin_ref[halo - j:halo - j + q, :]
    xwin_ref[0:halo, :] = xwin_ref[q:q + halo, :]
    xc_ref[...] = _silu(acc)

    dt = _softplus(dt_ref[0] + dtb_ref[...])
    a = dt * (-jnp.exp(alog_ref[...]))
    row = lax.broadcasted_iota(jnp.int32, (q, q), 0)
    col = lax.broadcasted_iota(jnp.int32, (q, q), 1)
    causal = row >= col
    tril = jnp.where(causal, 1.0, 0.0).astype(F32)
    acum2 = jnp.dot(tril, a, preferred_element_type=F32, precision=lax.Precision.HIGHEST) * LOG2E
    acum2_t = acum2.T
    dt_t = dt.T
    row2_t = acum2_t - jnp.log2(dt_t)
    a_last2_t = jnp.broadcast_to(acum2_t[:, q - 1:q], acum2_t.shape)
    w_t = jnp.exp2(a_last2_t - acum2_t) * dt_t
    cdecay_t = jnp.exp2(a_last2_t)

    first = lax.broadcasted_iota(jnp.int32, (q, 2 * p), 1) < p

    for g in range(SSD_GROUPS):
        b_g = xc_ref[:, d_inner + g * n:d_inner + (g + 1) * n]
        c_g = xc_ref[:, d_inner + (SSD_GROUPS + g) * n:d_inner + (SSD_GROUPS + g + 1) * n]
        bt_g = b_g.T
        c_bf = c_g.astype(BF16)
        cb = jnp.dot(c_bf, bt_g.astype(BF16), preferred_element_type=F32)
        for j in range(pairs_per_group):
            pair = g * pairs_per_group + j
            h0 = 2 * pair
            cols = slice(pair * 2 * p, (pair + 1) * 2 * p)
            x_pair = xc_ref[:, cols]
            m_parts, btw_parts, ea_parts = [], [], []
            for hh in (h0, h0 + 1):
                col_a2 = jnp.broadcast_to(acum2[:, hh:hh + 1], (q, q))
                lmat = jnp.exp2(jnp.where(causal, col_a2 - row2_t[hh:hh + 1, :], NEG_BIG))
                m_parts.append((cb * lmat).astype(BF16))
                btw_parts.append((bt_g * w_t[hh:hh + 1, :]).astype(BF16))
                ea_parts.append(jnp.exp2(col_a2))
            xb = x_pair.astype(BF16)
            zb = jnp.zeros_like(xb)
            xz = jnp.concatenate([jnp.where(first, xb, zb), jnp.where(first, zb, xb)], axis=0)
            lhs = jnp.concatenate([jnp.concatenate(m_parts, axis=1),
                                   jnp.concatenate(btw_parts, axis=1)], axis=0)
            r1 = jnp.dot(lhs, xz, preferred_element_type=F32)
            prev = state_ref[pair]
            y_off = (jnp.dot(c_bf, prev.astype(BF16), preferred_element_type=F32)
                     * jnp.where(first, ea_parts[0], ea_parts[1]))
            cd = jnp.where(first[0:1, :], cdecay_t[h0:h0 + 1, :], cdecay_t[h0 + 1:h0 + 2, :])
            state_ref[pair] = prev * cd + r1[q:q + n, :]
            y = r1[0:q, :] + y_off + dsk_ref[:, cols] * x_pair
            y_ref[:, cols] = y * _silu(z_ref[0, :, cols])

    for g in range(SSD_GROUPS):
        gs = slice(g * group_w, (g + 1) * group_w)
        o_ref[0, :, gs] = (_rms(y_ref[:, gs]) * ng_ref[:, gs]).astype(o_ref.dtype)


def _ssd(xbc, z, dtp, conv_w, conv_b, dt_bias, a_log, d_skip, norm_g, *, n_heads):
    b, l, cch = xbc.shape
    d_inner = z.shape[-1]
    q = SSD_CHUNK
    assert SSD_STATE == q == 2 * SSD_HEAD_DIM == V7X_LANES
    conv_k = conv_w.shape[0]
    n_pairs = n_heads // 2
    kern = functools.partial(_ssd_kernel, n_heads=n_heads, d_inner=d_inner, conv_k=conv_k)
    return pl.pallas_call(
        kern,
        grid=(b, l // q),
        in_specs=[
            pl.BlockSpec((1, q, cch), lambda i, j: (i, j, 0)),
            pl.BlockSpec((1, q, d_inner), lambda i, j: (i, j, 0)),
            pl.BlockSpec((1, q, V7X_LANES), lambda i, j: (i, j, 0)),
            _const_spec(conv_w.shape),
            _const_spec(conv_b.shape),
            _const_spec(dt_bias.shape),
            _const_spec(a_log.shape),
            _const_spec(d_skip.shape),
            _const_spec(norm_g.shape),
        ],
        out_specs=pl.BlockSpec((1, q, d_inner), lambda i, j: (i, j, 0)),
        out_shape=jax.ShapeDtypeStruct((b, l, d_inner), BF16),
        scratch_shapes=[
            pltpu.VMEM((q + V7X_SUBLANES, cch), F32),
            pltpu.VMEM((q, cch), F32),
            pltpu.VMEM((n_pairs, SSD_STATE, 2 * SSD_HEAD_DIM), F32),
            pltpu.VMEM((q, d_inner), F32),
        ],
        compiler_params=pltpu.CompilerParams(
            dimension_semantics=("arbitrary", "arbitrary"), vmem_limit_bytes=40 * MIB),
        name="ssd",
    )(xbc, z, dtp, conv_w, conv_b, dt_bias, a_log, d_skip, norm_g)


def _diff_attn_kernel(slopes_ref, q_ref, k_ref, v_ref, lq1_ref, lk1_ref, lq2_ref, lk2_ref, sg_ref,
                      o_ref, *, blk, lam_init):
    d = DA_HEAD_DIM
    hw = 2 * d
    n_blk = q_ref.shape[1] // blk
    slope2 = slopes_ref[pl.program_id(1)] * LOG2E

    kv_grp = KV_BLOCKS_PER_STEP * blk
    first = lax.broadcasted_iota(jnp.int32, (blk, hw), 1) < d
    lane = lax.broadcasted_iota(jnp.int32, (kv_grp, hw), 1)
    rest = slope2 * lax.broadcasted_iota(jnp.int32, (kv_grp, hw), 0).astype(F32)
    k_bias = jnp.zeros((kv_grp, hw), F32)
    for i in range(N_BIAS_TERMS):
        piece = rest.astype(BF16).astype(F32)
        k_bias = jnp.where(lane == i, piece, k_bias)
        rest = rest - piece
    k_bias = k_bias.astype(BF16)
    q_ones = jnp.where(lax.broadcasted_iota(jnp.int32, (2 * blk, hw), 1) < N_BIAS_TERMS, 1.0, 0.0).astype(BF16)
    rr = lax.broadcasted_iota(jnp.int32, (2 * blk, blk), 0)
    cc = lax.broadcasted_iota(jnp.int32, (2 * blk, blk), 1)
    tri = jnp.where(rr < blk, rr, rr - blk) >= cc
    ones = jnp.ones((kv_grp, hw), BF16)
    lam = (jnp.exp(jnp.sum(lq1_ref[...] * lk1_ref[...], axis=1, keepdims=True))
           - jnp.exp(jnp.sum(lq2_ref[...] * lk2_ref[...], axis=1, keepdims=True)) + lam_init)

    for qi in range(n_blk):
        qv = q_ref[0, qi * blk:(qi + 1) * blk, :]
        zq = jnp.zeros_like(qv)
        qs = jnp.concatenate([jnp.where(first, qv, zq), jnp.where(first, zq, qv)], axis=0)
        qx = jnp.concatenate([qs, q_ones], axis=1)
        m = None
        acc = None
        prev_start = 0
        for k0 in range(0, qi + 1, KV_BLOCKS_PER_STEP):
            n_kv = min(KV_BLOCKS_PER_STEP, qi + 1 - k0)
            start, rows = k0 * blk, n_kv * blk
            kx = jnp.concatenate([k_ref[0, start:start + rows, :], k_bias[0:rows]], axis=1)
            vx = jnp.concatenate([v_ref[0, start:start + rows, :], ones[0:rows]], axis=1)
            t = lax.dot_general(qx, kx, (((1,), (1,)), ((), ())), preferred_element_type=F32)
            if k0 + n_kv == qi + 1:
                t_diag = jnp.where(tri, t[:, rows - blk:rows], NEG_BIG)
                t = t_diag if rows == blk else jnp.concatenate([t[:, 0:rows - blk], t_diag], axis=1)
            bm = jnp.max(t, axis=1, keepdims=True)
            if m is None:
                m_new = bm
            else:
                m_old = m - slope2 * float(start - prev_start)
                m_new = jnp.maximum(m_old, bm)
            p = jnp.exp2(t - m_new).astype(BF16)
            pv = jnp.dot(p, vx, preferred_element_type=F32)
            acc = pv if m is None else jnp.exp2(m_old - m_new) * acc + pv
            m = m_new
            prev_start = start
        o1 = acc[0:blk, 0:hw] / acc[0:blk, hw:2 * hw]
        o2 = acc[blk:2 * blk, 0:hw] / acc[blk:2 * blk, hw:2 * hw]
        out = o1 - lam * o2
        out = _rms(out) * sg_ref[...] * (1.0 - lam_init)
        o_ref[0, qi * blk:(qi + 1) * blk, :] = out.astype(o_ref.dtype)


def _diff_attn(q, k, v, slopes, lq1, lk1, lq2, lk2, subln_g, *, blk, lam_init):
    b, l, w = q.shape
    hw = 2 * DA_HEAD_DIM
    n_heads = w // hw
    kern = functools.partial(_diff_attn_kernel, blk=blk, lam_init=lam_init)
    seq_spec = pl.BlockSpec((1, l, hw), lambda i, j: (i, 0, j))
    return pl.pallas_call(
        kern,
        grid=(b, n_heads),
        in_specs=[
            pl.BlockSpec(memory_space=pltpu.SMEM),
            seq_spec, seq_spec, seq_spec,
            _const_spec(lq1.shape), _const_spec(lk1.shape), _const_spec(lq2.shape), _const_spec(lk2.shape),
            _const_spec(subln_g.shape),
        ],
        out_specs=seq_spec,
        out_shape=jax.ShapeDtypeStruct((b, l, w), BF16),
        compiler_params=pltpu.CompilerParams(
            dimension_semantics=("arbitrary", "arbitrary"), vmem_limit_bytes=32 * MIB),
        name="diff_attn",
    )(slopes, q, k, v, lq1, lk1, lq2, lk2, subln_g)


def _merge_kernel(h_ref, ys_ref, ya_ref, gl_ref, gb_ref, ws_ref, wa_ref, wo_ref, pg_ref, o_ref):
    d = h_ref.shape[1]
    y_ssd = jnp.dot(ys_ref[...], ws_ref[...], preferred_element_type=F32)
    y_att = jnp.dot(ya_ref[...], wa_ref[...], preferred_element_type=F32)
    gates = jax.nn.sigmoid(gl_ref[...] + gb_ref[...])
    merged = (gates[:, 0:d] * y_ssd + gates[:, d:2 * d] * y_att).astype(BF16)
    mo = jnp.dot(merged, wo_ref[...], preferred_element_type=F32)
    o_ref[...] = h_ref[...] + _rms(mo) * pg_ref[...]


def _merge(h, ys, ya, gl, gate_b, w_ssd, w_da, w_out, post_g, *, tm):
    t, d = h.shape
    return pl.pallas_call(
        _merge_kernel,
        grid=(t // tm,),
        in_specs=[
            pl.BlockSpec((tm, d), lambda i: (i, 0)),
            pl.BlockSpec((tm, ys.shape[1]), lambda i: (i, 0)),
            pl.BlockSpec((tm, ya.shape[1]), lambda i: (i, 0)),
            pl.BlockSpec((tm, gl.shape[1]), lambda i: (i, 0)),
            _const_spec(gate_b.shape),
            _const_spec(w_ssd.shape),
            _const_spec(w_da.shape),
            _const_spec(w_out.shape),
            _const_spec(post_g.shape),
        ],
        out_specs=pl.BlockSpec((tm, d), lambda i: (i, 0)),
        out_shape=jax.ShapeDtypeStruct((t, d), F32),
        compiler_params=pltpu.CompilerParams(
            dimension_semantics=("arbitrary",), vmem_limit_bytes=48 * MIB),
        name="merge",
    )(h, ys, ya, gl, gate_b, w_ssd, w_da, w_out, post_g)


def kernel(x, ffn1_pre_g, ffn1_w_gate, ffn1_w_up, ffn1_w_down, ffn1_post_g, mix_pre_g, w_in, gate_b, ssd_conv_w, ssd_conv_b, ssd_dt_bias, ssd_A_log, ssd_D, ssd_norm_g, ssd_w_branch, da_lambda_q1, da_lambda_k1, da_lambda_q2, da_lambda_k2, da_subln_g, da_w_branch, w_out, mix_post_g, ffn2_pre_g, ffn2_w_gate, ffn2_w_up, ffn2_w_down, ffn2_post_g):
    b, l, d = x.shape
    depth = w_in.shape[0]
    d_inner = ssd_norm_g.shape[1]
    n_ssd_heads = ssd_D.shape[1]
    conv_ch = ssd_conv_w.shape[2]
    da_width = da_w_branch.shape[1]
    n_da_heads = da_width // (2 * DA_HEAD_DIM)
    sizes = [d_inner, conv_ch, n_ssd_heads, da_width, da_width, da_width, N_BRANCH * d]
    offs = [0]
    for s in sizes:
        offs.append(offs[-1] + s)
    slopes = jnp.asarray([2.0 ** (-8.0 * (i + 1) / n_da_heads) for i in range(n_da_heads)], F32)
    q_scale = LOG2E / math.sqrt(DA_HEAD_DIM)
    pad_h = V7X_LANES - n_ssd_heads

    def row(v):
        return v.reshape(1, -1).astype(F32)

    h = x.reshape(b * l, d)
    for i in range(depth):
        lam_init = 0.8 - 0.6 * math.exp(-0.3 * i)
        h = _ffn(h, row(ffn1_pre_g[i]), ffn1_w_gate[i].astype(BF16), ffn1_w_up[i].astype(BF16),
                 ffn1_w_down[i].astype(BF16), row(ffn1_post_g[i]), tm=512)

        ws = [w_in[i][:, offs[j]:offs[j + 1]].astype(BF16) for j in range(len(sizes))]
        ws[2] = jnp.pad(ws[2], ((0, 0), (0, pad_h)))
        z, xbc, dtp, qq, kk, vv, gl = _in_proj(h, row(mix_pre_g[i]), ws, tm=256, q_scale=q_scale)

        y_ssd = _ssd(xbc.reshape(b, l, conv_ch), z.reshape(b, l, d_inner), dtp.reshape(b, l, V7X_LANES),
                     ssd_conv_w[i].astype(F32), row(ssd_conv_b[i]),
                     jnp.pad(row(ssd_dt_bias[i]), ((0, 0), (0, pad_h))),
                     jnp.pad(row(ssd_A_log[i]), ((0, 0), (0, pad_h))),
                     row(jnp.repeat(ssd_D[i], SSD_HEAD_DIM)), row(ssd_norm_g[i]), n_heads=n_ssd_heads)

        y_att = _diff_attn(qq.reshape(b, l, da_width), kk.reshape(b, l, da_width), vv.reshape(b, l, da_width),
                           slopes, row(da_lambda_q1[i]), row(da_lambda_k1[i]), row(da_lambda_q2[i]),
                           row(da_lambda_k2[i]), row(da_subln_g[i]), blk=256, lam_init=lam_init)

        h = _merge(h, y_ssd.reshape(b * l, d_inner), y_att.reshape(b * l, da_width), gl, row(gate_b[i]),
                   ssd_w_branch[i].astype(BF16), da_w_branch[i].astype(BF16), w_out[i].astype(BF16),
                   row(mix_post_g[i]), tm=512)

        h = _ffn(h, row(ffn2_pre_g[i]), ffn2_w_gate[i].astype(BF16), ffn2_w_up[i].astype(BF16),
                 ffn2_w_down[i].astype(BF16), row(ffn2_post_g[i]), tm=512)
    return h.reshape(b, l, d)
```

```python
import functools
import math

import jax
import jax.numpy as jnp
from jax import lax
from jax.experimental import pallas as pl
from jax.experimental.pallas import tpu as pltpu

F32 = jnp.float32
BF16 = jnp.bfloat16

EPS = 1e-6
LOG2E = 1.4426950408889634

SSD_HEAD_DIM = 64
SSD_GROUPS = 4
SSD_STATE = 128
SSD_CHUNK = 128
DA_HEAD_DIM = 64
N_BRANCH = 2

V7X_LANES = 128
V7X_SUBLANES = 8
V7X_VMEM_BYTES = 64 * 1024 * 1024
MIB = 1024 * 1024

NEG_BIG = -1e30
KV_BLOCKS_PER_STEP = 1
N_BIAS_TERMS = 3


def _const_spec(shape):
    nd = len(shape)
    return pl.BlockSpec(shape, lambda *_: (0,) * nd, pipeline_mode=pl.Buffered(1))


def _rms(x):
    return x * lax.rsqrt(jnp.mean(x * x, axis=-1, keepdims=True) + EPS)


def _silu(x):
    hx = 0.5 * x
    return hx + hx * jnp.tanh(hx)


def _softplus(x):
    return jnp.maximum(x, 0.0) + jnp.log(1.0 + jnp.exp(-jnp.abs(x)))


def _ffn_kernel(x_ref, pre_g_ref, wg_ref, wu_ref, wd_ref, post_g_ref, o_ref):
    x = x_ref[...]
    xn = (_rms(x) * pre_g_ref[...]).astype(BF16)
    g = jnp.dot(xn, wg_ref[...], preferred_element_type=F32)
    u = jnp.dot(xn, wu_ref[...], preferred_element_type=F32)
    a = (_silu(g) * u).astype(BF16)
    y = jnp.dot(a, wd_ref[...], preferred_element_type=F32)
    o_ref[...] = x + 0.5 * (_rms(y) * post_g_ref[...])


def _ffn(x, pre_g, wg, wu, wd, post_g, *, tm):
    t, d = x.shape
    f = wg.shape[1]
    vmem = (3 * d * f * 2) + 4 * tm * d * 4 + 3 * tm * f * 4 + 8 * MIB
    return pl.pallas_call(
        _ffn_kernel,
        grid=(t // tm,),
        in_specs=[
            pl.BlockSpec((tm, d), lambda i: (i, 0)),
            _const_spec((1, d)),
            _const_spec((d, f)),
            _const_spec((d, f)),
            _const_spec((f, d)),
            _const_spec((1, d)),
        ],
        out_specs=pl.BlockSpec((tm, d), lambda i: (i, 0)),
        out_shape=jax.ShapeDtypeStruct((t, d), F32),
        compiler_params=pltpu.CompilerParams(
            dimension_semantics=("arbitrary",), vmem_limit_bytes=min(vmem, V7X_VMEM_BYTES - 4 * MIB)),
        name="ffn",
    )(x, pre_g, wg, wu, wd, post_g)


def _split_cast_kernel(w_ref, *o_refs, offs, widths):
    for o_ref, off, width in zip(o_refs, offs, widths):
        piece = w_ref[:, off:off + width].astype(o_ref.dtype)
        if width == o_ref.shape[1]:
            o_ref[...] = piece
        else:
            o_ref[...] = jnp.zeros(o_ref.shape, o_ref.dtype)
            o_ref[:, 0:width] = piece


def _split_cast(w, offs, *, pad_to, tr):
    k, n = w.shape
    widths = [offs[j + 1] - offs[j] for j in range(len(offs) - 1)]
    out_w = [pad_to.get(j, wd) for j, wd in enumerate(widths)]
    return pl.pallas_call(
        functools.partial(_split_cast_kernel, offs=tuple(offs[:-1]), widths=tuple(widths)),
        grid=(k // tr,),
        in_specs=[pl.BlockSpec((tr, n), lambda i: (i, 0))],
        out_specs=[pl.BlockSpec((tr, wd), lambda i: (i, 0)) for wd in out_w],
        out_shape=[jax.ShapeDtypeStruct((k, wd), BF16) for wd in out_w],
        compiler_params=pltpu.CompilerParams(dimension_semantics=("arbitrary",), vmem_limit_bytes=32 * MIB),
        name="split_cast",
    )(w)


def _in_proj_kernel(h_ref, g_ref, wz_ref, wx_ref, wdt_ref, wq_ref, wk_ref, wv_ref, wgl_ref,
                    z_ref, xbc_ref, dt_ref, q_ref, k_ref, v_ref, gl_ref, *, q_scale):
    u = (_rms(h_ref[...]) * g_ref[...]).astype(BF16)

    def proj(w_ref):
        return jnp.dot(u, w_ref[...], preferred_element_type=F32)

    z_ref[...] = proj(wz_ref)
    xbc_ref[...] = proj(wx_ref)
    dt_ref[...] = proj(wdt_ref)
    q_ref[...] = (proj(wq_ref) * q_scale).astype(BF16)
    k_ref[...] = proj(wk_ref).astype(BF16)
    v_ref[...] = proj(wv_ref).astype(BF16)
    gl_ref[...] = proj(wgl_ref)


def _in_proj(h, g, ws, *, tm, q_scale):
    t, d = h.shape
    widths = [w.shape[1] for w in ws]
    out_dtypes = [F32, F32, F32, BF16, BF16, BF16, F32]
    w_bytes = sum(d * n * 2 for n in widths)
    o_bytes = sum(tm * n * jnp.dtype(dt).itemsize for n, dt in zip(widths, out_dtypes))
    vmem = w_bytes + 2 * o_bytes + 2 * tm * d * 4 + tm * max(widths) * 4 + 8 * MIB
    return pl.pallas_call(
        functools.partial(_in_proj_kernel, q_scale=q_scale),
        grid=(t // tm,),
        in_specs=[pl.BlockSpec((tm, d), lambda i: (i, 0)), _const_spec((1, d))]
        + [_const_spec(w.shape) for w in ws],
        out_specs=[pl.BlockSpec((tm, n), lambda i: (i, 0)) for n in widths],
        out_shape=[jax.ShapeDtypeStruct((t, n), dt) for n, dt in zip(widths, out_dtypes)],
        compiler_params=pltpu.CompilerParams(
            dimension_semantics=("arbitrary",), vmem_limit_bytes=min(vmem, V7X_VMEM_BYTES - 4 * MIB)),
        name="in_proj",
    )(h, g, *ws)


def _ssd_kernel(xbc_ref, z_ref, dt_ref, cw_ref, cb_ref, dtb_ref, alog_ref, dsk_ref, ng_ref,
                o_ref, xwin_ref, xc_ref, state_ref, y_ref, *, n_heads, d_inner, conv_k):
    q = SSD_CHUNK
    p = SSD_HEAD_DIM
    n = SSD_STATE
    halo = V7X_SUBLANES
    heads_per_group = n_heads // SSD_GROUPS
    pairs_per_group = heads_per_group // 2
    group_w = d_inner // SSD_GROUPS
    c = pl.program_id(1)

    @pl.when(c == 0)
    def _():
        xwin_ref[0:halo, :] = jnp.zeros((halo, xwin_ref.shape[1]), F32)
        state_ref[...] = jnp.zeros(state_ref.shape, F32)

    xwin_ref[halo:halo + q, :] = xbc_ref[0]
    acc = cb_ref[...] + cw_ref[conv_k - 1:conv_k, :] * xbc_ref[0]
    for j in range(1, conv_k):
        acc = acc + cw_ref[conv_k - 1 - j:conv_k - j, :] * xwin_ref[halo - j:halo - j + q, :]
    xwin_ref[0:halo, :] = xwin_ref[q:q + halo, :]
    xc_ref[...] = _silu(acc)

    dt = _softplus(dt_ref[0] + dtb_ref[...])
    a = dt * (-jnp.exp(alog_ref[...]))
    row = lax.broadcasted_iota(jnp.int32, (q, q), 0)
    col = lax.broadcasted_iota(jnp.int32, (q, q), 1)
    causal = row >= col
    tril = jnp.where(causal, 1.0, 0.0).astype(F32)
    acum2 = jnp.dot(tril, a, preferred_element_type=F32, precision=lax.Precision.HIGHEST) * LOG2E
    acum2_t = acum2.T
    dt_t = dt.T
    row2_t = acum2_t - jnp.log2(dt_t)
    a_last2_t = jnp.broadcast_to(acum2_t[:, q - 1:q], acum2_t.shape)
    w_t = jnp.exp2(a_last2_t - acum2_t) * dt_t
    cdecay_t = jnp.exp2(a_last2_t)

    first = lax.broadcasted_iota(jnp.int32, (q, 2 * p), 1) < p

    for g in range(SSD_GROUPS):
        b_g = xc_ref[:, d_inner + g * n:d_inner + (g + 1) * n]
        c_g = xc_ref[:, d_inner + (SSD_GROUPS + g) * n:d_inner + (SSD_GROUPS + g + 1) * n]
        bt_g = b_g.T
        c_bf = c_g.astype(BF16)
        cb = jnp.dot(c_bf, bt_g.astype(BF16), preferred_element_type=F32)
        for j in range(pairs_per_group):
            pair = g * pairs_per_group + j
            h0 = 2 * pair
            cols = slice(pair * 2 * p, (pair + 1) * 2 * p)
            x_pair = xc_ref[:, cols]
            m_parts, btw_parts, ea_parts = [], [], []
            for hh in (h0, h0 + 1):
                col_a2 = jnp.broadcast_to(acum2[:, hh:hh + 1], (q, q))
                lmat = jnp.exp2(jnp.where(causal, col_a2 - row2_t[hh:hh + 1, :], NEG_BIG))
                m_parts.append((cb * lmat).astype(BF16))
                btw_parts.append((bt_g * w_t[hh:hh + 1, :]).astype(BF16))
                ea_parts.append(jnp.exp2(col_a2))
            xb = x_pair.astype(BF16)
            zb = jnp.zeros_like(xb)
            xz = jnp.concatenate([jnp.where(first, xb, zb), jnp.where(first, zb, xb)], axis=0)
            lhs = jnp.concatenate([jnp.concatenate(m_parts, axis=1),
                                   jnp.concatenate(btw_parts, axis=1)], axis=0)
            r1 = jnp.dot(lhs, xz, preferred_element_type=F32)
            prev = state_ref[pair]
            y_off = (jnp.dot(c_bf, prev.astype(BF16), preferred_element_type=F32)
                     * jnp.where(first, ea_parts[0], ea_parts[1]))
            cd = jnp.where(first[0:1, :], cdecay_t[h0:h0 + 1, :], cdecay_t[h0 + 1:h0 + 2, :])
            state_ref[pair] = prev * cd + r1[q:q + n, :]
            y = r1[0:q, :] + y_off + dsk_ref[:, cols] * x_pair
            y_ref[:, cols] = y * _silu(z_ref[0, :, cols])

    for g in range(SSD_GROUPS):
        gs = slice(g * group_w, (g + 1) * group_w)
        o_ref[0, :, gs] = (_rms(y_ref[:, gs]) * ng_ref[:, gs]).astype(o_ref.dtype)


def _ssd(xbc, z, dtp, conv_w, conv_b, dt_bias, a_log, d_skip, norm_g, *, n_heads):
    b, l, cch = xbc.shape
    d_inner = z.shape[-1]
    q = SSD_CHUNK
    assert SSD_STATE == q == 2 * SSD_HEAD_DIM == V7X_LANES
    conv_k = conv_w.shape[0]
    n_pairs = n_heads // 2
    kern = functools.partial(_ssd_kernel, n_heads=n_heads, d_inner=d_inner, conv_k=conv_k)
    return pl.pallas_call(
        kern,
        grid=(b, l // q),
        in_specs=[
            pl.BlockSpec((1, q, cch), lambda i, j: (i, j, 0)),
            pl.BlockSpec((1, q, d_inner), lambda i, j: (i, j, 0)),
            pl.BlockSpec((1, q, V7X_LANES), lambda i, j: (i, j, 0)),
            _const_spec(conv_w.shape),
            _const_spec(conv_b.shape),
            _const_spec(dt_bias.shape),
            _const_spec(a_log.shape),
            _const_spec(d_skip.shape),
            _const_spec(norm_g.shape),
        ],
        out_specs=pl.BlockSpec((1, q, d_inner), lambda i, j: (i, j, 0)),
        out_shape=jax.ShapeDtypeStruct((b, l, d_inner), BF16),
        scratch_shapes=[
            pltpu.VMEM((q + V7X_SUBLANES, cch), F32),
            pltpu.VMEM((q, cch), F32),
            pltpu.VMEM((n_pairs, SSD_STATE, 2 * SSD_HEAD_DIM), F32),
            pltpu.VMEM((q, d_inner), F32),
        ],
        compiler_params=pltpu.CompilerParams(
            dimension_semantics=("arbitrary", "arbitrary"), vmem_limit_bytes=40 * MIB),
        name="ssd",
    )(xbc, z, dtp, conv_w, conv_b, dt_bias, a_log, d_skip, norm_g)


def _diff_attn_kernel(slopes_ref, q_ref, k_ref, v_ref, lq1_ref, lk1_ref, lq2_ref, lk2_ref, sg_ref,
                      o_ref, *, blk, lam_init):
    d = DA_HEAD_DIM
    hw = 2 * d
    n_blk = q_ref.shape[1] // blk
    slope2 = slopes_ref[pl.program_id(1)] * LOG2E

    kv_grp = KV_BLOCKS_PER_STEP * blk
    first = lax.broadcasted_iota(jnp.int32, (blk, hw), 1) < d
    lane = lax.broadcasted_iota(jnp.int32, (kv_grp, hw), 1)
    rest = slope2 * lax.broadcasted_iota(jnp.int32, (kv_grp, hw), 0).astype(F32)
    k_bias = jnp.zeros((kv_grp, hw), F32)
    for i in range(N_BIAS_TERMS):
        piece = rest.astype(BF16).astype(F32)
        k_bias = jnp.where(lane == i, piece, k_bias)
        rest = rest - piece
    k_bias = k_bias.astype(BF16)
    q_ones = jnp.where(lax.broadcasted_iota(jnp.int32, (2 * blk, hw), 1) < N_BIAS_TERMS, 1.0, 0.0).astype(BF16)
    rr = lax.broadcasted_iota(jnp.int32, (2 * blk, blk), 0)
    cc = lax.broadcasted_iota(jnp.int32, (2 * blk, blk), 1)
    tri = jnp.where(rr < blk, rr, rr - blk) >= cc
    ones = jnp.ones((blk, hw), BF16)
    lam = (jnp.exp(jnp.sum(lq1_ref[...] * lk1_ref[...], axis=1, keepdims=True))
           - jnp.exp(jnp.sum(lq2_ref[...] * lk2_ref[...], axis=1, keepdims=True)) + lam_init)

    for qi in range(n_blk):
        qv = q_ref[0, qi * blk:(qi + 1) * blk, :]
        zq = jnp.zeros_like(qv)
        qs = jnp.concatenate([jnp.where(first, qv, zq), jnp.where(first, zq, qv)], axis=0)
        qx = jnp.concatenate([qs, q_ones], axis=1)
        m = None
        acc = None
        prev_start = 0
        for k0 in range(0, qi + 1, KV_BLOCKS_PER_STEP):
            n_kv = min(KV_BLOCKS_PER_STEP, qi + 1 - k0)
            start = k0 * blk
            ts = []
            for kb in range(n_kv):
                rows = slice(start + kb * blk, start + (kb + 1) * blk)
                kx = jnp.concatenate([k_ref[0, rows, :], k_bias[kb * blk:(kb + 1) * blk]], axis=1)
                t = lax.dot_general(qx, kx, (((1,), (1,)), ((), ())), preferred_element_type=F32)
                if k0 + kb == qi:
                    t = jnp.where(tri, t, NEG_BIG)
                ts.append(t)
            bm = jnp.max(ts[0], axis=1, keepdims=True)
            for t in ts[1:]:
                bm = jnp.maximum(bm, jnp.max(t, axis=1, keepdims=True))
            if m is None:
                m_new = bm
            else:
                m_old = m - slope2 * float(start - prev_start)
                m_new = jnp.maximum(m_old, bm)
            pv = None
            for kb, t in enumerate(ts):
                rows = slice(start + kb * blk, start + (kb + 1) * blk)
                vx = jnp.concatenate([v_ref[0, rows, :], ones], axis=1)
                part = jnp.dot(jnp.exp2(t - m_new).astype(BF16), vx, preferred_element_type=F32)
                pv = part if pv is None else pv + part
            acc = pv if m is None else jnp.exp2(m_old - m_new) * acc + pv
            m = m_new
            prev_start = start
        o1 = acc[0:blk, 0:hw] / acc[0:blk, hw:2 * hw]
        o2 = acc[blk:2 * blk, 0:hw] / acc[blk:2 * blk, hw:2 * hw]
        out = o1 - lam * o2
        out = _rms(out) * sg_ref[...] * (1.0 - lam_init)
        o_ref[0, qi * blk:(qi + 1) * blk, :] = out.astype(o_ref.dtype)


def _diff_attn(q, k, v, slopes, lq1, lk1, lq2, lk2, subln_g, *, blk, lam_init):
    b, l, w = q.shape
    hw = 2 * DA_HEAD_DIM
    n_heads = w // hw
    kern = functools.partial(_diff_attn_kernel, blk=blk, lam_init=lam_init)
    seq_spec = pl.BlockSpec((1, l, hw), lambda i, j: (i, 0, j))
    return pl.pallas_call(
        kern,
        grid=(b, n_heads),
        in_specs=[
            pl.BlockSpec(memory_space=pltpu.SMEM),
            seq_spec, seq_spec, seq_spec,
            _const_spec(lq1.shape), _const_spec(lk1.shape), _const_spec(lq2.shape), _const_spec(lk2.shape),
            _const_spec(subln_g.shape),
        ],
        out_specs=seq_spec,
        out_shape=jax.ShapeDtypeStruct((b, l, w), BF16),
        compiler_params=pltpu.CompilerParams(
            dimension_semantics=("arbitrary", "arbitrary"), vmem_limit_bytes=32 * MIB),
        name="diff_attn",
    )(slopes, q, k, v, lq1, lk1, lq2, lk2, subln_g)


def _merge_kernel(h_ref, ys_ref, ya_ref, gl_ref, gb_ref, ws_ref, wa_ref, wo_ref, pg_ref, o_ref):
    d = h_ref.shape[1]
    y_ssd = jnp.dot(ys_ref[...], ws_ref[...], preferred_element_type=F32)
    y_att = jnp.dot(ya_ref[...], wa_ref[...], preferred_element_type=F32)
    gates = jax.nn.sigmoid(gl_ref[...] + gb_ref[...])
    merged = (gates[:, 0:d] * y_ssd + gates[:, d:2 * d] * y_att).astype(BF16)
    mo = jnp.dot(merged, wo_ref[...], preferred_element_type=F32)
    o_ref[...] = h_ref[...] + _rms(mo) * pg_ref[...]


def _merge(h, ys, ya, gl, gate_b, w_ssd, w_da, w_out, post_g, *, tm):
    t, d = h.shape
    return pl.pallas_call(
        _merge_kernel,
        grid=(t // tm,),
        in_specs=[
            pl.BlockSpec((tm, d), lambda i: (i, 0)),
            pl.BlockSpec((tm, ys.shape[1]), lambda i: (i, 0)),
            pl.BlockSpec((tm, ya.shape[1]), lambda i: (i, 0)),
            pl.BlockSpec((tm, gl.shape[1]), lambda i: (i, 0)),
            _const_spec(gate_b.shape),
            _const_spec(w_ssd.shape),
            _const_spec(w_da.shape),
            _const_spec(w_out.shape),
            _const_spec(post_g.shape),
        ],
        out_specs=pl.BlockSpec((tm, d), lambda i: (i, 0)),
        out_shape=jax.ShapeDtypeStruct((t, d), F32),
        compiler_params=pltpu.CompilerParams(
            dimension_semantics=("arbitrary",), vmem_limit_bytes=48 * MIB),
        name="merge",
    )(h, ys, ya, gl, gate_b, w_ssd, w_da, w_out, post_g)


def kernel(x, ffn1_pre_g, ffn1_w_gate, ffn1_w_up, ffn1_w_down, ffn1_post_g, mix_pre_g, w_in, gate_b, ssd_conv_w, ssd_conv_b, ssd_dt_bias, ssd_A_log, ssd_D, ssd_norm_g, ssd_w_branch, da_lambda_q1, da_lambda_k1, da_lambda_q2, da_lambda_k2, da_subln_g, da_w_branch, w_out, mix_post_g, ffn2_pre_g, ffn2_w_gate, ffn2_w_up, ffn2_w_down, ffn2_post_g):
    b, l, d = x.shape
    depth = w_in.shape[0]
    d_inner = ssd_norm_g.shape[1]
    n_ssd_heads = ssd_D.shape[1]
    conv_ch = ssd_conv_w.shape[2]
    da_width = da_w_branch.shape[1]
    n_da_heads = da_width // (2 * DA_HEAD_DIM)
    sizes = [d_inner, conv_ch, n_ssd_heads, da_width, da_width, da_width, N_BRANCH * d]
    offs = [0]
    for s in sizes:
        offs.append(offs[-1] + s)
    slopes = jnp.asarray([2.0 ** (-8.0 * (i + 1) / n_da_heads) for i in range(n_da_heads)], F32)
    q_scale = LOG2E / math.sqrt(DA_HEAD_DIM)
    pad_h = V7X_LANES - n_ssd_heads

    def row(v):
        return v.reshape(1, -1).astype(F32)

    h = x.reshape(b * l, d)
    for i in range(depth):
        lam_init = 0.8 - 0.6 * math.exp(-0.3 * i)
        h = _ffn(h, row(ffn1_pre_g[i]), ffn1_w_gate[i].astype(BF16), ffn1_w_up[i].astype(BF16),
                 ffn1_w_down[i].astype(BF16), row(ffn1_post_g[i]), tm=512)

        ws = _split_cast(w_in[i], offs, pad_to={2: V7X_LANES}, tr=128)
        z, xbc, dtp, qq, kk, vv, gl = _in_proj(h, row(mix_pre_g[i]), ws, tm=256, q_scale=q_scale)

        y_ssd = _ssd(xbc.reshape(b, l, conv_ch), z.reshape(b, l, d_inner), dtp.reshape(b, l, V7X_LANES),
                     ssd_conv_w[i].astype(F32), row(ssd_conv_b[i]),
                     jnp.pad(row(ssd_dt_bias[i]), ((0, 0), (0, pad_h))),
                     jnp.pad(row(ssd_A_log[i]), ((0, 0), (0, pad_h))),
                     row(jnp.repeat(ssd_D[i], SSD_HEAD_DIM)), row(ssd_norm_g[i]), n_heads=n_ssd_heads)

        y_att = _diff_attn(qq.reshape(b, l, da_width), kk.reshape(b, l, da_width), vv.reshape(b, l, da_width),
                           slopes, row(da_lambda_q1[i]), row(da_lambda_k1[i]), row(da_lambda_q2[i]),
                           row(da_lambda_k2[i]), row(da_subln_g[i]), blk=256, lam_init=lam_init)

        h = _merge(h, y_ssd.reshape(b * l, d_inner), y_att.reshape(b * l, da_width), gl, row(gate_b[i]),
                   ssd_w_branch[i].astype(BF16), da_w_branch[i].astype(BF16), w_out[i].astype(BF16),
                   row(mix_post_g[i]), tm=512)

        h = _ffn(h, row(ffn2_pre_g[i]), ffn2_w_gate[i].astype(BF16), ffn2_w_up[i].astype(BF16),
                 ffn2_w_down[i].astype(BF16), row(ffn2_post_g[i]), tm=512)
    return h.reshape(b, l, d)
```
